```python
import math
import jax
import jax.numpy as jnp
from jax import lax
import numpy as np

D_MODEL = 1024
BATCH = 2
SEQ = 8192
DEPTH = 4
DEC_BATCH = 32
DEC_SEQ = 1
PAST_LEN = 8192
PAGE_SIZE = 128

HEAD_DIM = 64
MIX_WIDTH = D_MODEL
CONV_CH = MIX_WIDTH // 4
CONV_WIDTH = 31
LRU_CH = MIX_WIDTH // 4
LRU_BLOCKS = LRU_CH // HEAD_DIM
LRU_BLOCK = LRU_CH // LRU_BLOCKS
LRU_CONV_WIDTH = 4
LRU_C = 8.0
ATTN_WIDTH = MIX_WIDTH - CONV_CH - LRU_CH
ATTN_HEADS = ATTN_WIDTH // HEAD_DIM
ATTN_SCALE = HEAD_DIM ** -0.5
MOBA_BLOCK = 256
MOBA_TOPK = 3
Q_CHUNK = 32
REL_BUCKETS = 32
REL_MAX_DIST = 128
D_FF = 3 * D_MODEL
FFN_CONV_WIDTH = 3
DEEPNORM_ALPHA = (2 * DEPTH) ** 0.25
DEEPNORM_BETA = (8 * DEPTH) ** -0.25
LN_EPS = 1e-5
NEG_INF = -1e30

C_CONV_A = 0
C_CONV_G = C_CONV_A + CONV_CH
C_LRU_Y = C_CONV_G + CONV_CH
C_LRU_X = C_LRU_Y + LRU_CH
C_Q = C_LRU_X + LRU_CH
C_K = C_Q + ATTN_WIDTH
C_V = C_K + ATTN_WIDTH
IN_COLS = C_V + ATTN_WIDTH

kernel_name = 'hybrid_conv_rglru_moba_decoder_step'


def _layer_norm(x, g, b):
    xf = x.astype(jnp.float32)
    mu = jnp.mean(xf, axis=-1, keepdims=True)
    var = jnp.mean(jnp.square(xf - mu), axis=-1, keepdims=True)
    y = (xf - mu) * lax.rsqrt(var + LN_EPS) * g.astype(jnp.float32) + b.astype(jnp.float32)
    return y.astype(x.dtype)


def _causal_dwconv(x, buf, w, b):
    xp = jnp.concatenate([buf.astype(x.dtype), x], axis=1)
    y = lax.conv_general_dilated(xp, w[:, None, :].astype(x.dtype), window_strides=(1,), padding='VALID',
                                 dimension_numbers=('NWC', 'WIO', 'NWC'), feature_group_count=x.shape[-1])
    return y + b.astype(x.dtype), xp[:, xp.shape[1] - (w.shape[0] - 1):]


def _rel_bucket(dist):
    n = jnp.maximum(dist, 0)
    max_exact = REL_BUCKETS // 2
    nf = jnp.maximum(n, 1).astype(jnp.float32)
    large = max_exact + (jnp.log(nf / max_exact) / math.log(REL_MAX_DIST / max_exact)
                         * (REL_BUCKETS - max_exact)).astype(jnp.int32)
    return jnp.where(n < max_exact, n, jnp.minimum(large, REL_BUCKETS - 1))


def _gather_blocks(blk, idx):
    per_head = jax.vmap(lambda kb, ix: kb[ix], in_axes=(2, 0))
    return jax.vmap(per_head, in_axes=(0, 0))(blk, idx)


def _moba_core(q, k_blk, v_blk, k_mean, q_pos, rel_bias):
    n_blk = k_blk.shape[1]
    own = q_pos // MOBA_BLOCK
    gate = jnp.einsum('bqhd,bnhd->bhqn', q.astype(jnp.float32), k_mean)
    past = jnp.arange(n_blk)[None, :] < own[:, None]
    gate = jnp.where(past[None, None], gate, NEG_INF)
    _, top_i = lax.top_k(gate, min(MOBA_TOPK, n_blk))
    bsz, heads, nq = top_i.shape[:3]
    own_b = jnp.broadcast_to(own[None, None, :, None], (bsz, heads, nq, 1)).astype(top_i.dtype)
    idx = jnp.concatenate([top_i, own_b], axis=-1)
    blk_ok = jnp.concatenate([top_i < own_b, jnp.ones(own_b.shape, bool)], axis=-1)
    k_sel = _gather_blocks(k_blk, idx)
    v_sel = _gather_blocks(v_blk, idx)
    k_pos = idx[..., None] * MOBA_BLOCK + jnp.arange(MOBA_BLOCK)
    dist = q_pos[None, None, :, None, None] - k_pos
    allowed = blk_ok[..., None] & (dist >= 0)
    bias = rel_bias.T.astype(jnp.float32)[jnp.arange(heads)[None, :, None, None, None], _rel_bucket(dist)]
    logits = jnp.einsum('bqhd,bhqjsd->bhqjs', q, k_sel, preferred_element_type=jnp.float32) * ATTN_SCALE + bias
    logits = jnp.where(allowed, logits, NEG_INF)
    probs = jax.nn.softmax(logits.reshape(bsz, heads, nq, -1), axis=-1).reshape(logits.shape)
    return jnp.einsum('bhqjs,bhqjsd->bqhd', probs.astype(v_sel.dtype), v_sel)


def _moba_attention(q, k_new, v_new, k_past, v_past, rel_bias):
    bsz, t, heads, hd = q.shape
    past_len = k_past.shape[1]
    total = past_len + t
    n_blk = -(-total // MOBA_BLOCK)
    zpad = jnp.zeros((bsz, n_blk * MOBA_BLOCK - total, heads, hd), k_new.dtype)
    k_all = jnp.concatenate([k_past.astype(k_new.dtype), k_new, zpad], axis=1).reshape(bsz, n_blk, MOBA_BLOCK, heads, hd)
    v_all = jnp.concatenate([v_past.astype(v_new.dtype), v_new, zpad], axis=1).reshape(bsz, n_blk, MOBA_BLOCK, heads, hd)
    k_mean = jnp.mean(k_all, axis=2, dtype=jnp.float32)
    q_pos = past_len + jnp.arange(t, dtype=jnp.int32)
    qc = Q_CHUNK if t % Q_CHUNK == 0 else t
    nq = t // qc
    if nq == 1:
        return _moba_core(q, k_all, v_all, k_mean, q_pos, rel_bias)
    qs = q.reshape(bsz, nq, qc, heads, hd).transpose(1, 0, 2, 3, 4)
    ps = q_pos.reshape(nq, qc)
    out = lax.map(lambda a: _moba_core(a[0], k_all, v_all, k_mean, a[1], rel_bias), (qs, ps))
    return out.transpose(1, 0, 2, 3, 4).reshape(bsz, t, heads, hd)


def _rg_lru(x, h0, wa, ba, wx, bx, lam):
    bsz, t, c = x.shape
    xb = x.reshape(bsz, t, LRU_BLOCKS, LRU_BLOCK)
    r = jax.nn.sigmoid(jnp.einsum('btki,kij->btkj', xb, wa).reshape(bsz, t, c) + ba)
    i = jax.nn.sigmoid(jnp.einsum('btki,kij->btkj', xb, wx).reshape(bsz, t, c) + bx)
    log_a = -LRU_C * r.astype(jnp.float32) * jax.nn.softplus(-lam.astype(jnp.float32))
    a = jnp.exp(log_a)
    u = jnp.sqrt(-jnp.expm1(2.0 * log_a)) * (i * x).astype(jnp.float32)

    def step(h, au):
        h = au[0] * h + au[1]
        return h, h

    h_last, hs = lax.scan(step, h0.astype(jnp.float32), (jnp.swapaxes(a, 0, 1), jnp.swapaxes(u, 0, 1)))
    return jnp.swapaxes(hs, 0, 1).astype(x.dtype), h_last.astype(h0.dtype)


def _mixer(x, conv_buf, lru_buf, lru_h, k_past, v_past, w_in, conv_dw_w, conv_dw_b, conv_ln_g, conv_ln_b,
           lru_conv_w, lru_conv_b, lru_wa, lru_ba, lru_wx, lru_bx, lru_lambda, rel_bias, w_out):
    bsz, t, _ = x.shape
    z = x @ w_in
    glu = z[..., C_CONV_A:C_CONV_G] * jax.nn.sigmoid(z[..., C_CONV_G:C_LRU_Y])
    c, conv_buf = _causal_dwconv(glu, conv_buf, conv_dw_w, conv_dw_b)
    out_a = jax.nn.silu(_layer_norm(c, conv_ln_g, conv_ln_b))
    gate_b = jax.nn.gelu(z[..., C_LRU_Y:C_LRU_X])
    xr, lru_buf = _causal_dwconv(z[..., C_LRU_X:C_Q], lru_buf, lru_conv_w, lru_conv_b)
    h, lru_h = _rg_lru(xr, lru_h, lru_wa, lru_ba, lru_wx, lru_bx, lru_lambda)
    out_b = gate_b * h
    q = z[..., C_Q:C_K].reshape(bsz, t, ATTN_HEADS, HEAD_DIM)
    k = z[..., C_K:C_V].reshape(bsz, t, ATTN_HEADS, HEAD_DIM)
    v = z[..., C_V:IN_COLS].reshape(bsz, t, ATTN_HEADS, HEAD_DIM)
    out_c = _moba_attention(q, k, v, k_past, v_past, rel_bias).reshape(bsz, t, ATTN_WIDTH)
    y = jnp.concatenate([out_a, out_b, out_c], axis=-1) @ w_out
    return y, conv_buf, lru_buf, lru_h, k, v


def _conv_ffn(x, buf, w_up, w_gate, dw_w, dw_b, w_down):
    u, buf = _causal_dwconv(x @ w_up, buf, dw_w, dw_b)
    return (jax.nn.gelu(u) * (x @ w_gate)) @ w_down, buf


def _trunk(x, conv_bufs, lru_bufs, lru_hs, ffn_bufs, past_kv, p):
    ks, vs, cbs, lbs, lhs, fbs = [], [], [], [], [], []
    for l in range(DEPTH):
        k_past, v_past = past_kv(l)
        mix, cb, lb, lh, k, v = _mixer(x, conv_bufs[l], lru_bufs[l], lru_hs[l], k_past, v_past,
                                       p['w_in'][l], p['conv_dw_w'][l], p['conv_dw_b'][l], p['conv_ln_g'][l],
                                       p['conv_ln_b'][l], p['lru_conv_w'][l], p['lru_conv_b'][l], p['lru_wa'][l],
                                       p['lru_ba'][l], p['lru_wx'][l], p['lru_bx'][l], p['lru_lambda'][l],
                                       p['rel_bias'], p['w_out'][l])
        x = _layer_norm(DEEPNORM_ALPHA * x + mix, p['ln1_g'][l], p['ln1_b'][l])
        f, fb = _conv_ffn(x, ffn_bufs[l], p['ffn_w_up'][l], p['ffn_w_gate'][l], p['ffn_dw_w'][l],
                          p['ffn_dw_b'][l], p['ffn_w_down'][l])
        x = _layer_norm(DEEPNORM_ALPHA * x + f, p['ln2_g'][l], p['ln2_b'][l])
        ks.append(k); vs.append(v); cbs.append(cb); lbs.append(lb); lhs.append(lh); fbs.append(fb)
    return (x, jnp.stack(ks), jnp.stack(vs), jnp.stack(cbs), jnp.stack(lbs), jnp.stack(lhs), jnp.stack(fbs))


def setup_inputs(seed: int = 0) -> dict:
    key = jax.random.key(seed)
    ks = jax.random.split(key, 32)
    n_pages = PAST_LEN // PAGE_SIZE
    n_used = DEC_BATCH * n_pages
    n_pool = (5 * n_used + 3) // 4

    def nrm(k, shape, s=1.0):
        return s * jax.random.normal(k, shape, jnp.float32)

    perm = jax.random.permutation(ks[4], n_pool)[:n_used]
    u = jax.random.uniform(ks[19], (DEPTH, LRU_CH), jnp.float32, 0.9, 0.999)
    s = u ** (1.0 / LRU_C)
    return {
        'x_prompt': nrm(ks[0], (BATCH, SEQ, D_MODEL)),
        'x_sample': nrm(ks[1], (DEC_BATCH, DEC_SEQ, D_MODEL)),
        'cache_k': nrm(ks[2], (DEPTH, n_pool, PAGE_SIZE, ATTN_HEADS, HEAD_DIM)),
        'cache_v': nrm(ks[3], (DEPTH, n_pool, PAGE_SIZE, ATTN_HEADS, HEAD_DIM)),
        'page_table': perm.reshape(DEC_BATCH, n_pages).astype(jnp.int32),
        'state_conv': nrm(ks[5], (DEPTH, DEC_BATCH, CONV_WIDTH - 1, CONV_CH), 0.5),
        'state_lru_conv': nrm(ks[6], (DEPTH, DEC_BATCH, LRU_CONV_WIDTH - 1, LRU_CH)),
        'state_lru_h': nrm(ks[7], (DEPTH, DEC_BATCH, LRU_CH), 0.5),
        'state_ffn_conv': nrm(ks[8], (DEPTH, DEC_BATCH, FFN_CONV_WIDTH - 1, D_FF)),
        'w_in': nrm(ks[9], (DEPTH, D_MODEL, IN_COLS), D_MODEL ** -0.5),
        'conv_dw_w': nrm(ks[10], (DEPTH, CONV_WIDTH, CONV_CH), CONV_WIDTH ** -0.5),
        'conv_dw_b': nrm(ks[11], (DEPTH, CONV_CH), 0.01),
        'conv_ln_g': 1.0 + nrm(ks[12], (DEPTH, CONV_CH), 0.01),
        'conv_ln_b': nrm(ks[13], (DEPTH, CONV_CH), 0.01),
        'lru_conv_w': nrm(ks[14], (DEPTH, LRU_CONV_WIDTH, LRU_CH), LRU_CONV_WIDTH ** -0.5),
        'lru_conv_b': nrm(ks[15], (DEPTH, LRU_CH), 0.01),
        'lru_wa': nrm(ks[16], (DEPTH, LRU_BLOCKS, LRU_BLOCK, LRU_BLOCK), LRU_BLOCK ** -0.5),
        'lru_ba': nrm(ks[17], (DEPTH, LRU_CH), 0.01),
        'lru_wx': nrm(ks[18], (DEPTH, LRU_BLOCKS, LRU_BLOCK, LRU_BLOCK), LRU_BLOCK ** -0.5),
        'lru_bx': nrm(ks[20], (DEPTH, LRU_CH), 0.01),
        'lru_lambda': jnp.log(s) - jnp.log1p(-s),
        'rel_bias': nrm(ks[21], (REL_BUCKETS, ATTN_HEADS), 0.5),
        'w_out': nrm(ks[22], (DEPTH, MIX_WIDTH, D_MODEL), DEEPNORM_BETA * MIX_WIDTH ** -0.5),
        'ln1_g': 1.0 + nrm(ks[23], (DEPTH, D_MODEL), 0.01),
        'ln1_b': nrm(ks[24], (DEPTH, D_MODEL), 0.01),
        'ffn_w_up': nrm(ks[25], (DEPTH, D_MODEL, D_FF), D_MODEL ** -0.5),
        'ffn_w_gate': nrm(ks[26], (DEPTH, D_MODEL, D_FF), D_MODEL ** -0.5),
        'ffn_dw_w': nrm(ks[27], (DEPTH, FFN_CONV_WIDTH, D_FF), FFN_CONV_WIDTH ** -0.5),
        'ffn_dw_b': nrm(ks[28], (DEPTH, D_FF), 0.01),
        'ffn_w_down': nrm(ks[29], (DEPTH, D_FF, D_MODEL), DEEPNORM_BETA * D_FF ** -0.5),
        'ln2_g': 1.0 + nrm(ks[30], (DEPTH, D_MODEL), 0.01),
        'ln2_b': nrm(ks[31], (DEPTH, D_MODEL), 0.01),
    }


def reference(x_prompt, x_sample, cache_k, cache_v, page_table, state_conv, state_lru_conv, state_lru_h,
              state_ffn_conv, w_in, conv_dw_w, conv_dw_b, conv_ln_g, conv_ln_b, lru_conv_w, lru_conv_b,
              lru_wa, lru_ba, lru_wx, lru_bx, lru_lambda, rel_bias, w_out, ln1_g, ln1_b, ffn_w_up,
              ffn_w_gate, ffn_dw_w, ffn_dw_b, ffn_w_down, ln2_g, ln2_b):
    p = {'w_in': w_in, 'conv_dw_w': conv_dw_w, 'conv_dw_b': conv_dw_b, 'conv_ln_g': conv_ln_g,
         'conv_ln_b': conv_ln_b, 'lru_conv_w': lru_conv_w, 'lru_conv_b': lru_conv_b, 'lru_wa': lru_wa,
         'lru_ba': lru_ba, 'lru_wx': lru_wx, 'lru_bx': lru_bx, 'lru_lambda': lru_lambda,
         'rel_bias': rel_bias, 'w_out': w_out, 'ln1_g': ln1_g, 'ln1_b': ln1_b, 'ffn_w_up': ffn_w_up,
         'ffn_w_gate': ffn_w_gate, 'ffn_dw_w': ffn_dw_w, 'ffn_dw_b': ffn_dw_b, 'ffn_w_down': ffn_w_down,
         'ln2_g': ln2_g, 'ln2_b': ln2_b}

    nb_p = x_prompt.shape[0]
    dt = x_prompt.dtype
    zkv = jnp.zeros((nb_p, 0, ATTN_HEADS, HEAD_DIM), dt)
    (y_prompt, k_p, v_p, conv_p, lruc_p, lruh_p, ffn_p) = _trunk(
        x_prompt,
        jnp.zeros((DEPTH, nb_p, CONV_WIDTH - 1, CONV_CH), dt),
        jnp.zeros((DEPTH, nb_p, LRU_CONV_WIDTH - 1, LRU_CH), dt),
        jnp.zeros((DEPTH, nb_p, LRU_CH), dt),
        jnp.zeros((DEPTH, nb_p, FFN_CONV_WIDTH - 1, D_FF), dt),
        lambda l: (zkv, zkv), p)

    n_seq = page_table.shape[0]

    def sample_past(l):
        k_past = cache_k[l, page_table].reshape(n_seq, -1, ATTN_HEADS, HEAD_DIM)
        v_past = cache_v[l, page_table].reshape(n_seq, -1, ATTN_HEADS, HEAD_DIM)
        return k_past, v_past

    (y_sample, k_s, v_s, conv_s, lruc_s, lruh_s, ffn_s) = _trunk(
        x_sample, state_conv, state_lru_conv, state_lru_h, state_ffn_conv, sample_past, p)

    return (y_prompt, y_sample, k_p, v_p, k_s, v_s, conv_p, conv_s, lruc_p, lruc_s, lruh_p, lruh_s, ffn_p, ffn_s)
```

```python
import functools
import math

import jax
import jax.numpy as jnp
from jax import lax
from jax.experimental import pallas as pl
from jax.experimental.pallas import tpu as pltpu

F32 = jnp.float32
BF16 = jnp.bfloat16
I32 = jnp.int32

D_MODEL = 1024
HEAD_DIM = 64
CONV_CH = 256
CONV_WIDTH = 31
LRU_CH = 256
LRU_CONV_WIDTH = 4
LRU_C = 8.0
ATTN_WIDTH = 512
ATTN_HEADS = 8
HEAD_PAIRS = ATTN_HEADS // 2
MOBA_BLOCK = 256
MOBA_TOPK = 3
REL_BUCKETS = 32
REL_MAX_DIST = 128
D_FF = 3 * D_MODEL
FFN_CONV_WIDTH = 3
PAGE_SIZE = 128
PAGES_PER_BLOCK = MOBA_BLOCK // PAGE_SIZE
DEPTH = 4
DEEPNORM_ALPHA = (2 * DEPTH) ** 0.25
LN_EPS = 1e-5
NEG_INF = -1e30
Q_SCALE = HEAD_DIM ** -0.5

LANES = 128
SUBLANES = 8
BF16_ROWS = 16

ROW_TILE = 512
FF_TILE = 512
CONV_CHUNK = 64
CONV_HALO = 32
LRU_HALO = 8
KV_CHUNK_PAGES = 16


def _sigmoid(x):
    return 1.0 / (1.0 + jnp.exp(-x))


def _gelu_tanh(x):
    c = math.sqrt(2.0 / math.pi)
    return x * (0.5 * (1.0 + jnp.tanh(c * (x + 0.044715 * (x * x * x)))))


def _layer_norm(x, g, b):
    mu = jnp.mean(x, axis=-1, keepdims=True)
    xc = x - mu
    var = jnp.mean(xc * xc, axis=-1, keepdims=True)
    return xc * lax.rsqrt(var + LN_EPS) * g + b


def _lru_coeffs(xr, wa_ref, ba_ref, wx_ref, bx_ref, lam_ref):
    xb = xr.astype(BF16)
    r = _sigmoid(jnp.dot(xb, wa_ref[...], preferred_element_type=F32) + ba_ref[...])
    i = _sigmoid(jnp.dot(xb, wx_ref[...], preferred_element_type=F32) + bx_ref[...])
    nl = -lam_ref[...]
    softplus = jnp.maximum(nl, 0.0) + jnp.log1p(jnp.exp(-jnp.abs(nl)))
    log_a = (-LRU_C) * r * softplus
    a = jnp.exp(log_a)
    u = jnp.sqrt(-jnp.tanh(log_a) * (a * a + 1.0)) * (i * xr)
    return a, u


def _rel_bucket(dist):
    n = jnp.maximum(dist, 0)
    max_exact = REL_BUCKETS // 2
    nf = jnp.maximum(n, 1).astype(F32)
    large = max_exact + (jnp.log(nf / max_exact) / math.log(REL_MAX_DIST / max_exact)
                         * (REL_BUCKETS - max_exact)).astype(I32)
    return jnp.where(n < max_exact, n, jnp.minimum(large, REL_BUCKETS - 1))


def _bias_lookup(rb_ref, bucket, h):
    val = jnp.zeros(bucket.shape, F32)
    for k in range(REL_BUCKETS):
        val = jnp.where(bucket == k, rb_ref[k, h], val)
    return val


def _bias_tables_kernel(rb_ref, t0_ref, t1_ref, bs_ref):
    h = pl.program_id(0)
    kk = lax.broadcasted_iota(I32, (MOBA_BLOCK, MOBA_BLOCK), 0)
    qq = lax.broadcasted_iota(I32, (MOBA_BLOCK, MOBA_BLOCK), 1)
    d0 = qq - kk
    t0_ref[0] = jnp.where(d0 >= 0, _bias_lookup(rb_ref, _rel_bucket(d0), h), NEG_INF)
    t1_ref[0] = _bias_lookup(rb_ref, _rel_bucket(d0 + MOBA_BLOCK), h)
    ds = MOBA_BLOCK - lax.broadcasted_iota(I32, (SUBLANES, MOBA_BLOCK), 1)
    bs_ref[0] = _bias_lookup(rb_ref, _rel_bucket(ds), h)


def _bias_tables(rel_bias):
    tbl = jax.ShapeDtypeStruct((ATTN_HEADS, MOBA_BLOCK, MOBA_BLOCK), F32)
    return pl.pallas_call(
        _bias_tables_kernel,
        grid=(ATTN_HEADS,),
        in_specs=[pl.BlockSpec(memory_space=pltpu.SMEM)],
        out_specs=[pl.BlockSpec((1, MOBA_BLOCK, MOBA_BLOCK), lambda h: (h, 0, 0)),
                   pl.BlockSpec((1, MOBA_BLOCK, MOBA_BLOCK), lambda h: (h, 0, 0)),
                   pl.BlockSpec((1, SUBLANES, MOBA_BLOCK), lambda h: (h, 0, 0))],
        out_shape=[tbl, tbl, jax.ShapeDtypeStruct((ATTN_HEADS, SUBLANES, MOBA_BLOCK), F32)],
        name="bias_tables",
    )(rel_bias)


def _inproj_prompt_kernel(x_ref, w_ref, zc_ref, zl_ref, kt_ref, vt_ref, qtb_ref, kb_ref, vtb_ref, km_ref):
    xb = x_ref[...].astype(BF16)
    tm = xb.shape[0]

    def col(c):
        return jnp.dot(xb, w_ref[:, c * ATTN_WIDTH:(c + 1) * ATTN_WIDTH], preferred_element_type=F32)

    zc_ref[...] = col(0)
    zl_ref[...] = col(1)
    qtb_ref[0] = (col(2) * Q_SCALE).T.astype(BF16)
    k = col(3)
    kt_ref[0] = k.T
    kb_ref[...] = k.astype(BF16)
    for r in range(tm // MOBA_BLOCK):
        km_ref[r] = jnp.sum(k[r * MOBA_BLOCK:(r + 1) * MOBA_BLOCK], axis=0, keepdims=True) * (1.0 / MOBA_BLOCK)
    vt = col(4).T
    vt_ref[0] = vt
    for r in range(tm // MOBA_BLOCK):
        vtb_ref[0, r] = vt[:, r * MOBA_BLOCK:(r + 1) * MOBA_BLOCK].astype(BF16)


def _inproj_prompt(x, w, nbatch, seq):
    m = x.shape[0]
    tm = ROW_TILE
    nb = seq // tm
    nblk = seq // MOBA_BLOCK
    rows = lambda i: (i, 0)
    tcols = lambda i: (i // nb, 0, i % nb)
    return pl.pallas_call(
        _inproj_prompt_kernel,
        grid=(m // tm,),
        in_specs=[pl.BlockSpec((tm, D_MODEL), rows),
                  pl.BlockSpec(w.shape, lambda i: (0, 0))],
        out_specs=[pl.BlockSpec((tm, ATTN_WIDTH), rows),
                   pl.BlockSpec((tm, ATTN_WIDTH), rows),
                   pl.BlockSpec((1, ATTN_WIDTH, tm), tcols),
                   pl.BlockSpec((1, ATTN_WIDTH, tm), tcols),
                   pl.BlockSpec((1, ATTN_WIDTH, tm), tcols),
                   pl.BlockSpec((tm, ATTN_WIDTH), rows),
                   pl.BlockSpec((1, tm // MOBA_BLOCK, ATTN_WIDTH, MOBA_BLOCK), lambda i: (i // nb, i % nb, 0, 0)),
                   pl.BlockSpec((tm // MOBA_BLOCK, 1, ATTN_WIDTH), lambda i: (i, 0, 0))],
        out_shape=[jax.ShapeDtypeStruct((m, ATTN_WIDTH), F32),
                   jax.ShapeDtypeStruct((m, ATTN_WIDTH), F32),
                   jax.ShapeDtypeStruct((nbatch, ATTN_WIDTH, seq), F32),
                   jax.ShapeDtypeStruct((nbatch, ATTN_WIDTH, seq), F32),
                   jax.ShapeDtypeStruct((nbatch, ATTN_WIDTH, seq), BF16),
                   jax.ShapeDtypeStruct((m, ATTN_WIDTH), BF16),
                   jax.ShapeDtypeStruct((nbatch, nblk, ATTN_WIDTH, MOBA_BLOCK), BF16),
                   jax.ShapeDtypeStruct((m // MOBA_BLOCK, 1, ATTN_WIDTH), F32)],
        compiler_params=pltpu.CompilerParams(dimension_semantics=("parallel",)),
        name="inproj_prompt",
    )(x, w)


def _inproj_sample_kernel(x_ref, w_ref, zc_ref, zl_ref, q_ref, k_ref, v_ref):
    xb = x_ref[...].astype(BF16)
    for c, o_ref in enumerate((zc_ref, zl_ref, q_ref, k_ref, v_ref)):
        o_ref[...] = jnp.dot(xb, w_ref[:, c * ATTN_WIDTH:(c + 1) * ATTN_WIDTH], preferred_element_type=F32)


def _inproj_sample(x, w):
    m = x.shape[0]
    out = jax.ShapeDtypeStruct((m, ATTN_WIDTH), F32)
    return pl.pallas_call(_inproj_sample_kernel, out_shape=[out] * 5, name="inproj_sample")(x, w)


def _mixer_prompt_kernel(zc_ref, zl_ref, cw_ref, cb_ref, lg_ref, lb_ref, lw_ref, lcb_ref,
                         wa_ref, ba_ref, wx_ref, bx_ref, lam_ref,
                         oab_ref, cst_ref, lst_ref, hst_ref,
                         glu_ext, xl_ext, h_sc):
    i = pl.program_id(1)
    tc = zc_ref.shape[0]

    @pl.when(i == 0)
    def _():
        glu_ext[0:CONV_HALO, :] = jnp.zeros((CONV_HALO, CONV_CH), F32)
        xl_ext[0:LRU_HALO, :] = jnp.zeros((LRU_HALO, LRU_CH), F32)
        h_sc[...] = jnp.zeros(h_sc.shape, F32)

    glu_ext[CONV_HALO:CONV_HALO + tc, :] = zc_ref[:, 0:CONV_CH] * _sigmoid(zc_ref[:, CONV_CH:2 * CONV_CH])
    first = CONV_HALO - (CONV_WIDTH - 1)
    for c in range(tc // CONV_CHUNK):
        acc = jnp.broadcast_to(cb_ref[...], (CONV_CHUNK, CONV_CH))
        for j in range(CONV_WIDTH):
            acc = acc + cw_ref[j:j + 1, :] * glu_ext[pl.ds(c * CONV_CHUNK + first + j, CONV_CHUNK), :]
        ln = _layer_norm(acc, lg_ref[...], lb_ref[...])
        oab_ref[c * CONV_CHUNK:(c + 1) * CONV_CHUNK, 0:CONV_CH] = ln * _sigmoid(ln)

    xl_ext[LRU_HALO:LRU_HALO + tc, :] = zl_ref[:, LRU_CH:2 * LRU_CH]
    lfirst = LRU_HALO - (LRU_CONV_WIDTH - 1)
    xr = jnp.broadcast_to(lcb_ref[...], (tc, LRU_CH))
    for j in range(LRU_CONV_WIDTH):
        xr = xr + lw_ref[j:j + 1, :] * xl_ext[pl.ds(lfirst + j, tc), :]
    a, u = _lru_coeffs(xr, wa_ref, ba_ref, wx_ref, bx_ref, lam_ref)
    row = lax.broadcasted_iota(I32, (tc, LRU_CH), 0)
    s = 1
    while s < tc:
        keep = row >= s
        u = jnp.where(keep, a * pltpu.roll(u, s, 0) + u, u)
        a = jnp.where(keep, a * pltpu.roll(a, s, 0), a)
        s *= 2
    h = a * h_sc[0:1, :] + u
    oab_ref[:, CONV_CH:CONV_CH + LRU_CH] = _gelu_tanh(zl_ref[:, 0:LRU_CH]) * h
    h_last = h[tc - 1:tc, :]

    @pl.when(i == pl.num_programs(1) - 1)
    def _():
        cst_ref[0] = glu_ext[pl.ds(tc + first, CONV_WIDTH - 1), :]
        lst_ref[0] = xl_ext[pl.ds(tc + lfirst, LRU_CONV_WIDTH - 1), :]
        hst_ref[0] = h_last

    glu_ext[0:CONV_HALO, :] = glu_ext[tc:tc + CONV_HALO, :]
    xl_ext[0:LRU_HALO, :] = xl_ext[tc:tc + LRU_HALO, :]
    h_sc[...] = jnp.broadcast_to(h_last, h_sc.shape)


def _mixer_prompt(zc, zl, p, nbatch, seq):
    tc = ROW_TILE
    nt = seq // tc
    rows = lambda b, i: (b * nt + i, 0)
    full = lambda a: pl.BlockSpec(a.shape, lambda b, i: (0,) * a.ndim)
    params = [p["conv_dw_w"], p["conv_dw_b"], p["conv_ln_g"], p["conv_ln_b"], p["lru_conv_w"], p["lru_conv_b"],
              p["wa"], p["lru_ba"], p["wx"], p["lru_bx"], p["lru_lambda"]]
    return pl.pallas_call(
        _mixer_prompt_kernel,
        grid=(nbatch, nt),
        in_specs=[pl.BlockSpec((tc, 2 * CONV_CH), rows), pl.BlockSpec((tc, 2 * LRU_CH), rows)]
        + [full(a) for a in params],
        out_specs=[pl.BlockSpec((tc, CONV_CH + LRU_CH), rows),
                   pl.BlockSpec((1, CONV_WIDTH - 1, CONV_CH), lambda b, i: (b, 0, 0)),
                   pl.BlockSpec((1, LRU_CONV_WIDTH - 1, LRU_CH), lambda b, i: (b, 0, 0)),
                   pl.BlockSpec((1, 1, LRU_CH), lambda b, i: (b, 0, 0))],
        out_shape=[jax.ShapeDtypeStruct((nbatch * seq, CONV_CH + LRU_CH), F32),
                   jax.ShapeDtypeStruct((nbatch, CONV_WIDTH - 1, CONV_CH), F32),
                   jax.ShapeDtypeStruct((nbatch, LRU_CONV_WIDTH - 1, LRU_CH), F32),
                   jax.ShapeDtypeStruct((nbatch, 1, LRU_CH), F32)],
        scratch_shapes=[pltpu.VMEM((tc + CONV_HALO, CONV_CH), F32),
                        pltpu.VMEM((tc + LRU_HALO, LRU_CH), F32),
                        pltpu.VMEM((SUBLANES, LRU_CH), F32)],
        compiler_params=pltpu.CompilerParams(dimension_semantics=("parallel", "arbitrary")),
        name="mixer_prompt",
    )(zc, zl, *params)


def _mixer_sample_kernel(zc_ref, zl_ref, cs_ref, ls_ref, h0_ref, cw_ref, cb_ref, lg_ref, lb_ref, lw_ref, lcb_ref,
                         wa_ref, ba_ref, wx_ref, bx_ref, lam_ref,
                         oab_ref, cst_ref, lst_ref, hst_ref):
    glu = zc_ref[:, 0:CONV_CH] * _sigmoid(zc_ref[:, CONV_CH:2 * CONV_CH])
    acc = cb_ref[...] + cw_ref[CONV_WIDTH - 1:CONV_WIDTH, :] * glu
    for j in range(CONV_WIDTH - 1):
        acc = acc + cw_ref[j:j + 1, :] * cs_ref[j]
        if j >= 1:
            cst_ref[j - 1] = cs_ref[j]
    cst_ref[CONV_WIDTH - 2] = glu
    ln = _layer_norm(acc, lg_ref[...], lb_ref[...])
    oab_ref[:, 0:CONV_CH] = ln * _sigmoid(ln)

    xl = zl_ref[:, LRU_CH:2 * LRU_CH]
    xr = lcb_ref[...] + lw_ref[LRU_CONV_WIDTH - 1:LRU_CONV_WIDTH, :] * xl
    for j in range(LRU_CONV_WIDTH - 1):
        xr = xr + lw_ref[j:j + 1, :] * ls_ref[j]
        if j >= 1:
            lst_ref[j - 1] = ls_ref[j]
    lst_ref[LRU_CONV_WIDTH - 2] = xl
    a, u = _lru_coeffs(xr, wa_ref, ba_ref, wx_ref, bx_ref, lam_ref)
    h = a * h0_ref[...] + u
    hst_ref[...] = h
    oab_ref[:, CONV_CH:CONV_CH + LRU_CH] = _gelu_tanh(zl_ref[:, 0:LRU_CH]) * h


def _mixer_sample(zc, zl, conv_state, lru_state, h0, p):
    ns = zc.shape[0]
    params = [p["conv_dw_w"], p["conv_dw_b"], p["conv_ln_g"], p["conv_ln_b"], p["lru_conv_w"], p["lru_conv_b"],
              p["wa"], p["lru_ba"], p["wx"], p["lru_bx"], p["lru_lambda"]]
    return pl.pallas_call(
        _mixer_sample_kernel,
        out_shape=[jax.ShapeDtypeStruct((ns, CONV_CH + LRU_CH), F32),
                   jax.ShapeDtypeStruct(conv_state.shape, F32),
                   jax.ShapeDtypeStruct(lru_state.shape, F32),
                   jax.ShapeDtypeStruct((ns, LRU_CH), F32)],
        name="mixer_sample",
    )(zc, zl, conv_state, lru_state, h0, *params)


def _select_prompt_kernel(rb_ref, km_ref, qt_ref, madd_ref):
    i = pl.program_id(1)
    nbp = madd_ref.shape[2]
    gate = jnp.dot(km_ref[0], qt_ref[0].astype(F32), preferred_element_type=F32,
                   precision=lax.Precision.HIGHEST)
    n = lax.broadcasted_iota(I32, (nbp, MOBA_BLOCK), 0)
    nf = n.astype(F32)
    past = n < i
    for h in range(ATTN_HEADS):
        g = jnp.where(past, gate[h * nbp:(h + 1) * nbp], NEG_INF)
        sel = jnp.zeros((nbp, MOBA_BLOCK), jnp.bool_)
        for _ in range(MOBA_TOPK):
            best = jnp.max(g, axis=0, keepdims=True)
            idx = jnp.min(jnp.where(g == best, nf, float(nbp)), axis=0, keepdims=True)
            pick = nf == idx
            sel = jnp.logical_or(sel, pick)
            g = jnp.where(pick, -jnp.inf, g)
        sel = jnp.logical_and(sel, past)
        far_bias = rb_ref[REL_BUCKETS - 1, h]
        madd_ref[0, h] = jnp.where(sel, jnp.where(n == i - 1, 0.0, far_bias), NEG_INF)


def _select_prompt(rel_bias, kmbd, qtb, nbp):
    nbatch, _, seq = qtb.shape
    nblk = seq // MOBA_BLOCK
    return pl.pallas_call(
        _select_prompt_kernel,
        grid=(nbatch, nblk),
        in_specs=[pl.BlockSpec(memory_space=pltpu.SMEM),
                  pl.BlockSpec((1, ATTN_HEADS * nbp, ATTN_WIDTH), lambda b, i: (b, 0, 0)),
                  pl.BlockSpec((1, ATTN_WIDTH, MOBA_BLOCK), lambda b, i: (b, 0, i))],
        out_specs=pl.BlockSpec((1, ATTN_HEADS, nbp, MOBA_BLOCK), lambda b, i: (b, 0, 0, i)),
        out_shape=jax.ShapeDtypeStruct((nbatch, ATTN_HEADS, nbp, seq), F32),
        compiler_params=pltpu.CompilerParams(dimension_semantics=("parallel", "arbitrary")),
        name="select_prompt",
    )(rel_bias, kmbd, qtb)


def _attn_prompt_kernel(qt_ref, k_ref, vt_ref, madd_ref, t0_ref, t1_ref, o_ref):
    i = pl.program_id(2)
    zeros = jnp.zeros((HEAD_DIM, MOBA_BLOCK), BF16)
    qt = qt_ref[0]
    qts = (jnp.concatenate([qt[0:HEAD_DIM], zeros], axis=0),
           jnp.concatenate([zeros, qt[HEAD_DIM:2 * HEAD_DIM]], axis=0))

    def kblock(j):
        return k_ref[pl.ds(pl.multiple_of(j * MOBA_BLOCK, MOBA_BLOCK), MOBA_BLOCK), :]

    def scores(kj, e):
        return jnp.dot(kj, qts[e], preferred_element_type=F32)

    def vblock(j, e):
        return vt_ref[0, j, e * HEAD_DIM:(e + 1) * HEAD_DIM, :]

    def update(carry, s, vt):
        m, l, acc = carry
        m_new = jnp.maximum(m, jnp.max(s, axis=0, keepdims=True))
        alpha = jnp.exp(m - m_new)
        p = jnp.exp(s - m_new)
        l = alpha * l + jnp.sum(p, axis=0, keepdims=True)
        acc = alpha * acc + jnp.dot(vt, p.astype(BF16), preferred_element_type=F32)
        return m_new, l, acc

    kd = kblock(i)
    carries = []
    for e in range(2):
        s = scores(kd, e) + t0_ref[e]
        m = jnp.max(s, axis=0, keepdims=True)
        p = jnp.exp(s - m)
        carries.append((m, jnp.sum(p, axis=0, keepdims=True),
                        jnp.dot(vblock(i, e), p.astype(BF16), preferred_element_type=F32)))

    jp = jnp.maximum(i - 1, 0)
    kp = kblock(jp)
    no_prev = jnp.where(i == 0, NEG_INF, 0.0)
    for e in range(2):
        s = scores(kp, e) + t1_ref[e] + (madd_ref[0, e, pl.ds(jp, 1), :] + no_prev)
        carries[e] = update(carries[e], s, vblock(jp, e))

    def far(j, cs):
        kj = kblock(j)
        out = []
        for e in range(2):
            s = scores(kj, e) + madd_ref[0, e, pl.ds(j, 1), :]
            out.append(update(cs[e], s, vblock(j, e)))
        return tuple(out)

    carries = lax.fori_loop(0, i - 1, far, tuple(carries))
    ot = jnp.concatenate([acc / l for (_, l, acc) in carries], axis=0)
    o_ref[...] = ot.T


def _attn_prompt(qtb, kb, vtb, madd, t0, t1):
    nbatch, _, seq = qtb.shape
    nblk = seq // MOBA_BLOCK
    nbp = madd.shape[2]
    return pl.pallas_call(
        _attn_prompt_kernel,
        grid=(nbatch, HEAD_PAIRS, nblk),
        in_specs=[pl.BlockSpec((1, 2 * HEAD_DIM, MOBA_BLOCK), lambda b, p, i: (b, p, i)),
                  pl.BlockSpec((seq, 2 * HEAD_DIM), lambda b, p, i: (b, p)),
                  pl.BlockSpec((1, nblk, 2 * HEAD_DIM, MOBA_BLOCK), lambda b, p, i: (b, 0, p, 0)),
                  pl.BlockSpec((1, 2, nbp, MOBA_BLOCK), lambda b, p, i: (b, p, 0, i)),
                  pl.BlockSpec((2, MOBA_BLOCK, MOBA_BLOCK), lambda b, p, i: (p, 0, 0)),
                  pl.BlockSpec((2, MOBA_BLOCK, MOBA_BLOCK), lambda b, p, i: (p, 0, 0))],
        out_specs=pl.BlockSpec((MOBA_BLOCK, 2 * HEAD_DIM), lambda b, p, i: (b * nblk + i, p)),
        out_shape=jax.ShapeDtypeStruct((nbatch * seq, ATTN_WIDTH), F32),
        compiler_params=pltpu.CompilerParams(dimension_semantics=("parallel", "parallel", "arbitrary")),
        name="attn_prompt",
    )(qtb, kb, vtb, madd, t0, t1)


def _make_kv_select_kernel(layer, n_pages):
    nchunk = n_pages // KV_CHUNK_PAGES
    blocks_per_chunk = KV_CHUNK_PAGES // PAGES_PER_BLOCK

    def page_copy(pt_ref, cache_ref, buf, sem, b, c, slot, p):
        page = pt_ref[b, c * KV_CHUNK_PAGES + p]
        return pltpu.make_async_copy(cache_ref.at[layer, page], buf.at[slot, p], sem.at[slot])

    def kernel(pt_ref, cache_ref, q_ref, sel_ref, buf, sem, kmt):
        b = pl.program_id(0)
        c = pl.program_id(1)
        step = b * nchunk + c
        slot = step % 2

        @pl.when(step == 0)
        def _():
            for p in range(KV_CHUNK_PAGES):
                page_copy(pt_ref, cache_ref, buf, sem, 0, 0, 0, p).start()

        @pl.when(step + 1 < pl.num_programs(0) * nchunk)
        def _():
            nxt = step + 1
            for p in range(KV_CHUNK_PAGES):
                page_copy(pt_ref, cache_ref, buf, sem, nxt // nchunk, nxt % nchunk, 1 - slot, p).start()

        for p in range(KV_CHUNK_PAGES):
            page_copy(pt_ref, cache_ref, buf, sem, b, c, slot, p).wait()

        @pl.when(c == 0)
        def _():
            kmt[...] = jnp.zeros(kmt.shape, F32)

        lane = lax.broadcasted_iota(I32, kmt.shape, 1)
        km = kmt[...]
        for kb in range(blocks_per_chunk):
            tot = buf[slot, PAGES_PER_BLOCK * kb]
            for r in range(1, PAGES_PER_BLOCK):
                tot = tot + buf[slot, PAGES_PER_BLOCK * kb + r]
            mean = jnp.sum(tot.reshape(ATTN_WIDTH, PAGE_SIZE), axis=1, keepdims=True) * (1.0 / MOBA_BLOCK)
            km = jnp.where(lane == c * blocks_per_chunk + kb, mean, km)
        kmt[...] = km

        @pl.when(c == nchunk - 1)
        def _():
            nblk = n_pages // PAGES_PER_BLOCK
            hrow = lax.broadcasted_iota(I32, (SUBLANES, ATTN_WIDTH), 0)
            hcol = lax.broadcasted_iota(I32, (SUBLANES, ATTN_WIDTH), 1) // HEAD_DIM
            qbd = jnp.where(hrow == hcol, jnp.broadcast_to(q_ref[0], (SUBLANES, ATTN_WIDTH)), 0.0)
            gate = jnp.dot(qbd, km, preferred_element_type=F32, precision=lax.Precision.HIGHEST)
            nf = lax.broadcasted_iota(I32, gate.shape, 1).astype(F32)
            g = jnp.where(nf < nblk, gate, -jnp.inf)
            for r in range(MOBA_TOPK):
                best = jnp.max(g, axis=1, keepdims=True)
                idx = jnp.min(jnp.where(g == best, nf, float(LANES)), axis=1, keepdims=True)
                sel_ref[0, r] = jnp.broadcast_to(idx, (SUBLANES, LANES)).astype(I32)
                g = jnp.where(nf == idx, -jnp.inf, g)

    return kernel


def _kv_select(page_table, cache_t, q3, layer):
    ns, n_pages = page_table.shape
    nchunk = n_pages // KV_CHUNK_PAGES
    return pl.pallas_call(
        _make_kv_select_kernel(layer, n_pages),
        grid_spec=pltpu.PrefetchScalarGridSpec(
            num_scalar_prefetch=1,
            grid=(ns, nchunk),
            in_specs=[pl.BlockSpec(memory_space=pl.ANY),
                      pl.BlockSpec((1, 1, ATTN_WIDTH), lambda b, c, pt: (b, 0, 0))],
            out_specs=pl.BlockSpec((1, MOBA_TOPK, SUBLANES, LANES), lambda b, c, pt: (b, 0, 0, 0)),
            scratch_shapes=[pltpu.VMEM((2, KV_CHUNK_PAGES, ATTN_HEADS, HEAD_DIM, PAGE_SIZE), F32),
                            pltpu.SemaphoreType.DMA((2,)),
                            pltpu.VMEM((ATTN_WIDTH, LANES), F32)]),
        out_shape=jax.ShapeDtypeStruct((ns, MOBA_TOPK, SUBLANES, LANES), I32),
        compiler_params=pltpu.CompilerParams(dimension_semantics=("arbitrary", "arbitrary")),
        name="kv_select",
    )(page_table, cache_t, q3)


def _make_attn_sample_kernel(layer, n_pages):
    nblk = n_pages // PAGES_PER_BLOCK
    nsel = MOBA_TOPK * MOBA_BLOCK

    def copies(pt_ref, sel_ref, ck_ref, cv_ref, kbuf, vbuf, sem, b, slot):
        out = []
        for h in range(ATTN_HEADS):
            for r in range(MOBA_TOPK):
                blk = sel_ref[b, h * MOBA_TOPK + r]
                for half in range(PAGES_PER_BLOCK):
                    page = pt_ref[b, blk * PAGES_PER_BLOCK + half]
                    dst = pl.ds(r * MOBA_BLOCK + half * PAGE_SIZE, PAGE_SIZE)
                    out.append(pltpu.make_async_copy(ck_ref.at[layer, page, h], kbuf.at[slot, h, :, dst], sem.at[0, slot]))
                    out.append(pltpu.make_async_copy(cv_ref.at[layer, page, h], vbuf.at[slot, h, :, dst], sem.at[1, slot]))
        return out

    def kernel(pt_ref, sel_ref, rb_ref, ck_ref, cv_ref, q_ref, kn_ref, vn_ref, bs_ref, o_ref, kbuf, vbuf, sem):
        b = pl.program_id(0)
        slot = b % 2

        @pl.when(b == 0)
        def _():
            for cp in copies(pt_ref, sel_ref, ck_ref, cv_ref, kbuf, vbuf, sem, 0, 0):
                cp.start()

        @pl.when(b + 1 < pl.num_programs(0))
        def _():
            for cp in copies(pt_ref, sel_ref, ck_ref, cv_ref, kbuf, vbuf, sem, b + 1, 1 - slot):
                cp.start()

        for cp in copies(pt_ref, sel_ref, ck_ref, cv_ref, kbuf, vbuf, sem, b, slot):
            cp.wait()

        q = q_ref[0] * Q_SCALE
        kn = kn_ref[0]
        vn = vn_ref[0]
        for h in range(ATTN_HEADS):
            hs = slice(h * HEAD_DIM, (h + 1) * HEAD_DIM)
            qh = q[:, hs]
            q8 = jnp.broadcast_to(qh, (SUBLANES, HEAD_DIM)).astype(BF16)
            s = jnp.dot(q8, kbuf[slot, h].astype(BF16), preferred_element_type=F32)
            far_bias = rb_ref[REL_BUCKETS - 1, h]
            bias = jnp.concatenate(
                [jnp.where(sel_ref[b, h * MOBA_TOPK + r] == nblk - 1, bs_ref[h], far_bias)
                 for r in range(MOBA_TOPK)], axis=1)
            s = s + bias
            s_new = jnp.sum(qh * kn[:, hs], axis=1, keepdims=True) + rb_ref[0, h]
            m = jnp.maximum(jnp.max(s, axis=1, keepdims=True), s_new)
            p = jnp.exp(s - m)
            p_new = jnp.exp(s_new - m)
            l = jnp.sum(p, axis=1, keepdims=True) + p_new
            o = lax.dot_general(p.astype(BF16), vbuf[slot, h].astype(BF16), (((1,), (1,)), ((), ())),
                                preferred_element_type=F32)
            o = (o + p_new * vn[:, hs]) / l
            o_ref[0, :, hs] = o[0:1]

    return kernel


def _attn_sample(page_table, sel, rel_bias, cache_kt, cache_vt, q3, k3, v3, bs, layer):
    ns, n_pages = page_table.shape
    nsel = MOBA_TOPK * MOBA_BLOCK
    row = pl.BlockSpec((1, 1, ATTN_WIDTH), lambda b, pt, sl: (b, 0, 0))
    return pl.pallas_call(
        _make_attn_sample_kernel(layer, n_pages),
        grid_spec=pltpu.PrefetchScalarGridSpec(
            num_scalar_prefetch=2,
            grid=(ns,),
            in_specs=[pl.BlockSpec(memory_space=pltpu.SMEM),
                      pl.BlockSpec(memory_space=pl.ANY),
                      pl.BlockSpec(memory_space=pl.ANY),
                      row, row, row,
                      pl.BlockSpec(bs.shape, lambda b, pt, sl: (0, 0, 0))],
            out_specs=row,
            scratch_shapes=[pltpu.VMEM((2, ATTN_HEADS, HEAD_DIM, nsel), F32),
                            pltpu.VMEM((2, ATTN_HEADS, HEAD_DIM, nsel), F32),
                            pltpu.SemaphoreType.DMA((2, 2))]),
        out_shape=jax.ShapeDtypeStruct((ns, 1, ATTN_WIDTH), F32),
        compiler_params=pltpu.CompilerParams(dimension_semantics=("arbitrary",)),
        name="attn_sample",
    )(page_table, sel, rel_bias, cache_kt, cache_vt, q3, k3, v3, bs)


def _outproj_kernel(ab_ref, c_ref, x_ref, w_ref, g_ref, b_ref, o_ref):
    split = CONV_CH + LRU_CH
    y = jnp.dot(ab_ref[...].astype(BF16), w_ref[0:split, :], preferred_element_type=F32)
    y = y + jnp.dot(c_ref[...].astype(BF16), w_ref[split:split + ATTN_WIDTH, :], preferred_element_type=F32)
    o_ref[...] = _layer_norm(DEEPNORM_ALPHA * x_ref[...] + y, g_ref[...], b_ref[...])


def _outproj(oab, oc, x, w, g, b):
    m = x.shape[0]
    tm = min(ROW_TILE, m)
    rows = lambda i: (i, 0)
    const = lambda a: pl.BlockSpec(a.shape, lambda i: (0, 0))
    return pl.pallas_call(
        _outproj_kernel,
        grid=(m // tm,),
        in_specs=[pl.BlockSpec((tm, CONV_CH + LRU_CH), rows), pl.BlockSpec((tm, ATTN_WIDTH), rows),
                  pl.BlockSpec((tm, D_MODEL), rows), const(w), const(g), const(b)],
        out_specs=pl.BlockSpec((tm, D_MODEL), rows),
        out_shape=jax.ShapeDtypeStruct((m, D_MODEL), F32),
        compiler_params=pltpu.CompilerParams(dimension_semantics=("parallel",)),
        name="outproj",
    )(oab, oc, x, w, g, b)


def _make_ffn_prompt_kernel(tiles_per_seq):
    halo = BF16_ROWS
    taps = FFN_CONV_WIDTH

    def kernel(x_ref, xh_ref, wu_ref, wg_ref, dw_ref, db_ref, wd_ref, g_ref, b_ref, o_ref, fst_ref, xb_sc, up_sc, acc_sc):
        i = pl.program_id(0)
        n = pl.program_id(1)
        tm = x_ref.shape[0]

        @pl.when(n == 0)
        def _():
            keep = jnp.where(i % tiles_per_seq == 0, 0.0, 1.0)
            xb_sc[0:halo, :] = (xh_ref[...] * keep).astype(BF16)
            xb_sc[halo:halo + tm, :] = x_ref[...].astype(BF16)
            acc_sc[...] = jnp.zeros(acc_sc.shape, F32)

        up_sc[...] = jnp.dot(xb_sc[...], wu_ref[...], preferred_element_type=F32)
        y = jnp.broadcast_to(db_ref[...], (tm, up_sc.shape[1]))
        for j in range(taps):
            y = y + dw_ref[j:j + 1, :] * up_sc[pl.ds(halo - (taps - 1) + j, tm), :]
        fst_ref[0] = up_sc[pl.ds(halo + tm - (taps - 1), taps - 1), :]
        gate = jnp.dot(xb_sc[halo:halo + tm, :], wg_ref[...], preferred_element_type=F32)
        acc_sc[...] += jnp.dot((_gelu_tanh(y) * gate).astype(BF16), wd_ref[...], preferred_element_type=F32)

        @pl.when(n == pl.num_programs(1) - 1)
        def _():
            o_ref[...] = _layer_norm(DEEPNORM_ALPHA * x_ref[...] + acc_sc[...], g_ref[...], b_ref[...])

    return kernel


def _ffn_prompt(x, p, nbatch, seq):
    m = x.shape[0]
    tm, tn = ROW_TILE, FF_TILE
    nt = seq // tm
    halo = BF16_ROWS
    hb = tm // halo
    y, tails = pl.pallas_call(
        _make_ffn_prompt_kernel(nt),
        grid=(m // tm, D_FF // tn),
        in_specs=[pl.BlockSpec((tm, D_MODEL), lambda i, n: (i, 0)),
                  pl.BlockSpec((halo, D_MODEL), lambda i, n: (jnp.maximum(i * hb - 1, 0), 0)),
                  pl.BlockSpec((D_MODEL, tn), lambda i, n: (0, n)),
                  pl.BlockSpec((D_MODEL, tn), lambda i, n: (0, n)),
                  pl.BlockSpec((FFN_CONV_WIDTH, tn), lambda i, n: (0, n)),
                  pl.BlockSpec((1, tn), lambda i, n: (0, n)),
                  pl.BlockSpec((tn, D_MODEL), lambda i, n: (n, 0)),
                  pl.BlockSpec((1, D_MODEL), lambda i, n: (0, 0)),
                  pl.BlockSpec((1, D_MODEL), lambda i, n: (0, 0))],
        out_specs=[pl.BlockSpec((tm, D_MODEL), lambda i, n: (i, 0)),
                   pl.BlockSpec((1, FFN_CONV_WIDTH - 1, tn), lambda i, n: (i, 0, n))],
        out_shape=[jax.ShapeDtypeStruct((m, D_MODEL), F32),
                   jax.ShapeDtypeStruct((m // tm, FFN_CONV_WIDTH - 1, D_FF), F32)],
        scratch_shapes=[pltpu.VMEM((tm + halo, D_MODEL), BF16),
                        pltpu.VMEM((tm + halo, tn), F32),
                        pltpu.VMEM((tm, D_MODEL), F32)],
        compiler_params=pltpu.CompilerParams(dimension_semantics=("arbitrary", "arbitrary")),
        name="ffn_prompt",
    )(x, x, p["w_up"], p["w_gate"], p["ffn_dw_w"], p["ffn_dw_b"], p["w_down"], p["ln2_g"], p["ln2_b"])
    return y, tails[nt - 1::nt]


def _ffn_sample_kernel(x_ref, st_ref, wu_ref, wg_ref, dw_ref, db_ref, wd_ref, g_ref, b_ref, o_ref, fst_ref, acc_sc):
    n = pl.program_id(0)
    taps = FFN_CONV_WIDTH

    @pl.when(n == 0)
    def _():
        acc_sc[...] = jnp.zeros(acc_sc.shape, F32)

    xb = x_ref[...].astype(BF16)
    up = jnp.dot(xb, wu_ref[...], preferred_element_type=F32)
    y = db_ref[...] + dw_ref[taps - 1:taps, :] * up
    for j in range(taps - 1):
        y = y + dw_ref[j:j + 1, :] * st_ref[j]
        if j >= 1:
            fst_ref[j - 1] = st_ref[j]
    fst_ref[taps - 2] = up
    gate = jnp.dot(xb, wg_ref[...], preferred_element_type=F32)
    acc_sc[...] += jnp.dot((_gelu_tanh(y) * gate).astype(BF16), wd_ref[...], preferred_element_type=F32)

    @pl.when(n == pl.num_programs(0) - 1)
    def _():
        o_ref[...] = _layer_norm(DEEPNORM_ALPHA * x_ref[...] + acc_sc[...], g_ref[...], b_ref[...])


def _ffn_sample(x, state, p):
    ns = x.shape[0]
    tn = FF_TILE
    return pl.pallas_call(
        _ffn_sample_kernel,
        grid=(D_FF // tn,),
        in_specs=[pl.BlockSpec((ns, D_MODEL), lambda n: (0, 0)),
                  pl.BlockSpec((FFN_CONV_WIDTH - 1, ns, tn), lambda n: (0, 0, n)),
                  pl.BlockSpec((D_MODEL, tn), lambda n: (0, n)),
                  pl.BlockSpec((D_MODEL, tn), lambda n: (0, n)),
                  pl.BlockSpec((FFN_CONV_WIDTH, tn), lambda n: (0, n)),
                  pl.BlockSpec((1, tn), lambda n: (0, n)),
                  pl.BlockSpec((tn, D_MODEL), lambda n: (n, 0)),
                  pl.BlockSpec((1, D_MODEL), lambda n: (0, 0)),
                  pl.BlockSpec((1, D_MODEL), lambda n: (0, 0))],
        out_specs=[pl.BlockSpec((ns, D_MODEL), lambda n: (0, 0)),
                   pl.BlockSpec((FFN_CONV_WIDTH - 1, ns, tn), lambda n: (0, 0, n))],
        out_shape=[jax.ShapeDtypeStruct((ns, D_MODEL), F32),
                   jax.ShapeDtypeStruct((FFN_CONV_WIDTH - 1, ns, D_FF), F32)],
        scratch_shapes=[pltpu.VMEM((ns, D_MODEL), F32)],
        compiler_params=pltpu.CompilerParams(dimension_semantics=("arbitrary",)),
        name="ffn_sample",
    )(x, state, p["w_up"], p["w_gate"], p["ffn_dw_w"], p["ffn_dw_b"], p["w_down"], p["ln2_g"], p["ln2_b"])


def _block_diag(w):
    nb, n, _ = w.shape
    eye = jnp.eye(nb, dtype=w.dtype)
    return (eye[:, None, :, None] * w[:, :, None, :]).reshape(nb * n, nb * n)


def _layer_params(l, w_in, conv_dw_w, conv_dw_b, conv_ln_g, conv_ln_b, lru_conv_w, lru_conv_b, lru_wa, lru_ba,
                  lru_wx, lru_bx, lru_lambda, w_out, ln1_g, ln1_b, ffn_w_up, ffn_w_gate, ffn_dw_w, ffn_dw_b,
                  ffn_w_down, ln2_g, ln2_b):
    row = lambda a: a[l][None, :]
    return {
        "w_in": w_in[l].astype(BF16), "conv_dw_w": conv_dw_w[l], "conv_dw_b": row(conv_dw_b),
        "conv_ln_g": row(conv_ln_g), "conv_ln_b": row(conv_ln_b), "lru_conv_w": lru_conv_w[l],
        "lru_conv_b": row(lru_conv_b), "wa": _block_diag(lru_wa[l]).astype(BF16), "lru_ba": row(lru_ba),
        "wx": _block_diag(lru_wx[l]).astype(BF16), "lru_bx": row(lru_bx), "lru_lambda": row(lru_lambda),
        "w_out": w_out[l].astype(BF16), "ln1_g": row(ln1_g), "ln1_b": row(ln1_b),
        "w_up": ffn_w_up[l].astype(BF16), "w_gate": ffn_w_gate[l].astype(BF16), "ffn_dw_w": ffn_dw_w[l],
        "ffn_dw_b": row(ffn_dw_b), "w_down": ffn_w_down[l].astype(BF16), "ln2_g": row(ln2_g), "ln2_b": row(ln2_b),
    }


def _block_diag_means(kmean, nbatch, nblk, nbp):
    km = kmean.reshape(nbatch, nblk, ATTN_HEADS, HEAD_DIM).transpose(0, 2, 1, 3)
    km = jnp.pad(km, ((0, 0), (0, 0), (0, nbp - nblk), (0, 0)))
    eye = jnp.eye(ATTN_HEADS, dtype=F32)
    bd = km[:, :, :, None, :] * eye[None, :, None, :, None]
    return bd.reshape(nbatch, ATTN_HEADS * nbp, ATTN_WIDTH)


def kernel(x_prompt, x_sample, cache_k, cache_v, page_table, state_conv, state_lru_conv, state_lru_h, state_ffn_conv, w_in, conv_dw_w, conv_dw_b, conv_ln_g, conv_ln_b, lru_conv_w, lru_conv_b, lru_wa, lru_ba, lru_wx, lru_bx, lru_lambda, rel_bias, w_out, ln1_g, ln1_b, ffn_w_up, ffn_w_gate, ffn_dw_w, ffn_dw_b, ffn_w_down, ln2_g, ln2_b):
    nbatch, seq, _ = x_prompt.shape
    ns = x_sample.shape[0]
    depth = w_in.shape[0]
    nblk = seq // MOBA_BLOCK
    nbp = -(-nblk // SUBLANES) * SUBLANES
    assert seq % ROW_TILE == 0 and x_sample.shape[1] == 1
    assert page_table.shape[1] % KV_CHUNK_PAGES == 0 and cache_k.shape[2] == PAGE_SIZE

    t0, t1, bs = _bias_tables(rel_bias)
    cache_kt = cache_k.transpose(0, 1, 3, 4, 2)
    cache_vt = cache_v.transpose(0, 1, 3, 4, 2)

    xp = x_prompt.reshape(nbatch * seq, D_MODEL)
    xs = x_sample.reshape(ns, D_MODEL)
    outs = {k: [] for k in ("k_p", "v_p", "k_s", "v_s", "conv_p", "conv_s", "lruc_p", "lruc_s",
                            "lruh_p", "lruh_s", "ffn_p", "ffn_s")}
    for l in range(depth):
        p = _layer_params(l, w_in, conv_dw_w, conv_dw_b, conv_ln_g, conv_ln_b, lru_conv_w, lru_conv_b, lru_wa,
                          lru_ba, lru_wx, lru_bx, lru_lambda, w_out, ln1_g, ln1_b, ffn_w_up, ffn_w_gate,
                          ffn_dw_w, ffn_dw_b, ffn_w_down, ln2_g, ln2_b)

        zc, zl, kt, vt, qtb, kb, vtb, kmean = _inproj_prompt(xp, p["w_in"], nbatch, seq)
        oab, cst, lst, hst = _mixer_prompt(zc, zl, p, nbatch, seq)
        madd = _select_prompt(rel_bias, _block_diag_means(kmean, nbatch, nblk, nbp), qtb, nbp)
        oc = _attn_prompt(qtb, kb, vtb, madd, t0, t1)
        x1 = _outproj(oab, oc, xp, p["w_out"], p["ln1_g"], p["ln1_b"])
        xp, fst = _ffn_prompt(x1, p, nbatch, seq)
        to_heads = lambda a: a.reshape(nbatch, ATTN_HEADS, HEAD_DIM, seq).transpose(0, 3, 1, 2)
        outs["k_p"].append(to_heads(kt))
        outs["v_p"].append(to_heads(vt))
        outs["conv_p"].append(cst)
        outs["lruc_p"].append(lst)
        outs["lruh_p"].append(hst.reshape(nbatch, LRU_CH))
        outs["ffn_p"].append(fst)

        zc, zl, q, k, v = _inproj_sample(xs, p["w_in"])
        oab, cst, lst, hst = _mixer_sample(zc, zl, state_conv[l].transpose(1, 0, 2),
                                           state_lru_conv[l].transpose(1, 0, 2), state_lru_h[l], p)
        q3, k3, v3 = (a.reshape(ns, 1, ATTN_WIDTH) for a in (q, k, v))
        sel = _kv_select(page_table, cache_kt, q3, l)
        sel = sel[:, :, :, 0].transpose(0, 2, 1).reshape(ns, ATTN_HEADS * MOBA_TOPK)
        oc = _attn_sample(page_table, sel, rel_bias, cache_kt, cache_vt, q3, k3, v3, bs, l)
        x1 = _outproj(oab, oc.reshape(ns, ATTN_WIDTH), xs, p["w_out"], p["ln1_g"], p["ln1_b"])
        xs, fst = _ffn_sample(x1, state_ffn_conv[l].transpose(1, 0, 2), p)
        outs["k_s"].append(k.reshape(ns, 1, ATTN_HEADS, HEAD_DIM))
        outs["v_s"].append(v.reshape(ns, 1, ATTN_HEADS, HEAD_DIM))
        outs["conv_s"].append(cst.transpose(1, 0, 2))
        outs["lruc_s"].append(lst.transpose(1, 0, 2))
        outs["lruh_s"].append(hst)
        outs["ffn_s"].append(fst.transpose(1, 0, 2))

    st = {k: jnp.stack(v) for k, v in outs.items()}
    return (xp.reshape(nbatch, seq, D_MODEL), xs.reshape(ns, 1, D_MODEL),
            st["k_p"], st["v_p"], st["k_s"], st["v_s"], st["conv_p"], st["conv_s"],
            st["lruc_p"], st["lruc_s"], st["lruh_p"], st["lruh_s"], st["ffn_p"], st["ffn_s"])
```

```python
import functools
import math

import jax
import jax.numpy as jnp
import numpy as np
from jax import lax
from jax.experimental import pallas as pl
from jax.experimental.pallas import tpu as pltpu

F32 = jnp.float32
BF16 = jnp.bfloat16
I32 = jnp.int32

D_MODEL = 1024
HEAD_DIM = 64
CONV_CH = 256
CONV_WIDTH = 31
LRU_CH = 256
LRU_CONV_WIDTH = 4
LRU_C = 8.0
ATTN_WIDTH = 512
ATTN_HEADS = 8
HEAD_PAIRS = ATTN_HEADS // 2
MOBA_BLOCK = 256
MOBA_TOPK = 3
REL_BUCKETS = 32
REL_MAX_DIST = 128
D_FF = 3 * D_MODEL
FFN_CONV_WIDTH = 3
PAGE_SIZE = 128
PAGES_PER_BLOCK = MOBA_BLOCK // PAGE_SIZE
DEPTH = 4
DEEPNORM_ALPHA = (2 * DEPTH) ** 0.25
LN_EPS = 1e-5
NEG_INF = -1e30
Q_SCALE = HEAD_DIM ** -0.5
LOG2E = math.log2(math.e)

LANES = 128
SUBLANES = 8
BF16_ROWS = 16
VT_ROWS = HEAD_DIM + BF16_ROWS

ROW_TILE = 512
FF_TILE = 512
CONV_CHUNK = 64
CONV_HALO = 32
LRU_HALO = 8
KV_CHUNK_PAGES = 16
FAR_GROUP = 4


def _sigmoid(x):
    return 1.0 / (1.0 + jnp.exp(-x))


def _gelu_tanh(x):
    c = math.sqrt(2.0 / math.pi)
    return x * (0.5 * (1.0 + jnp.tanh(c * (x + 0.044715 * (x * x * x)))))


def _layer_norm(x, g, b):
    mu = jnp.mean(x, axis=-1, keepdims=True)
    xc = x - mu
    var = jnp.mean(xc * xc, axis=-1, keepdims=True)
    return xc * lax.rsqrt(var + LN_EPS) * g + b


def _lru_coeffs(xr, wa_ref, ba_ref, wx_ref, bx_ref, lam_ref):
    xb = xr.astype(BF16)
    r = _sigmoid(jnp.dot(xb, wa_ref[...], preferred_element_type=F32) + ba_ref[...])
    i = _sigmoid(jnp.dot(xb, wx_ref[...], preferred_element_type=F32) + bx_ref[...])
    nl = -lam_ref[...]
    softplus = jnp.maximum(nl, 0.0) + jnp.log1p(jnp.exp(-jnp.abs(nl)))
    log_a = (-LRU_C) * r * softplus
    a = jnp.exp(log_a)
    u = jnp.sqrt(-jnp.tanh(log_a) * (a * a + 1.0)) * (i * xr)
    return a, u


def _rel_bucket(dist):
    n = jnp.maximum(dist, 0)
    max_exact = REL_BUCKETS // 2
    nf = jnp.maximum(n, 1).astype(F32)
    large = max_exact + (jnp.log(nf / max_exact) / math.log(REL_MAX_DIST / max_exact)
                         * (REL_BUCKETS - max_exact)).astype(I32)
    return jnp.where(n < max_exact, n, jnp.minimum(large, REL_BUCKETS - 1))


def _bias_lookup(rb_ref, bucket, h):
    val = jnp.zeros(bucket.shape, F32)
    for k in range(REL_BUCKETS):
        val = jnp.where(bucket == k, rb_ref[k, h], val)
    return val


def _bias_tables_kernel(rb_ref, t0_ref, t1_ref, bs_ref):
    h = pl.program_id(0)
    kk = lax.broadcasted_iota(I32, (MOBA_BLOCK, MOBA_BLOCK), 0)
    qq = lax.broadcasted_iota(I32, (MOBA_BLOCK, MOBA_BLOCK), 1)
    d0 = qq - kk
    t0_ref[0] = jnp.where(d0 >= 0, _bias_lookup(rb_ref, _rel_bucket(d0), h) * LOG2E, NEG_INF)
    t1_ref[0] = _bias_lookup(rb_ref, _rel_bucket(d0 + MOBA_BLOCK), h) * LOG2E
    ds = MOBA_BLOCK - lax.broadcasted_iota(I32, (SUBLANES, MOBA_BLOCK), 1)
    bs_ref[0] = _bias_lookup(rb_ref, _rel_bucket(ds), h)


def _bias_tables(rel_bias):
    tbl = jax.ShapeDtypeStruct((ATTN_HEADS, MOBA_BLOCK, MOBA_BLOCK), F32)
    return pl.pallas_call(
        _bias_tables_kernel,
        grid=(ATTN_HEADS,),
        in_specs=[pl.BlockSpec(memory_space=pltpu.SMEM)],
        out_specs=[pl.BlockSpec((1, MOBA_BLOCK, MOBA_BLOCK), lambda h: (h, 0, 0)),
                   pl.BlockSpec((1, MOBA_BLOCK, MOBA_BLOCK), lambda h: (h, 0, 0)),
                   pl.BlockSpec((1, SUBLANES, MOBA_BLOCK), lambda h: (h, 0, 0))],
        out_shape=[tbl, tbl, jax.ShapeDtypeStruct((ATTN_HEADS, SUBLANES, MOBA_BLOCK), F32)],
        name="bias_tables",
    )(rel_bias)


def _inproj_prompt_kernel(x_ref, w_ref, zc_ref, zl_ref, kt_ref, vt_ref, qtb_ref, kb_ref, vtb_ref, km_ref):
    xb = x_ref[...].astype(BF16)
    tm = xb.shape[0]

    def col(c):
        return jnp.dot(xb, w_ref[:, c * ATTN_WIDTH:(c + 1) * ATTN_WIDTH], preferred_element_type=F32)

    zc_ref[...] = col(0)
    zl_ref[...] = col(1)
    qtb_ref[0] = (col(2) * (Q_SCALE * LOG2E)).T.astype(BF16)
    k = col(3)
    kt_ref[0] = k.T
    kb_ref[...] = k.astype(BF16)
    for r in range(tm // MOBA_BLOCK):
        km_ref[r] = jnp.sum(k[r * MOBA_BLOCK:(r + 1) * MOBA_BLOCK], axis=0, keepdims=True) * (1.0 / MOBA_BLOCK)
    vt = col(4).T
    vt_ref[0] = vt
    ones = jnp.ones((VT_ROWS - HEAD_DIM, tm), BF16)
    for h in range(ATTN_HEADS):
        vtb_ref[0, h, 0:HEAD_DIM, :] = vt[h * HEAD_DIM:(h + 1) * HEAD_DIM].astype(BF16)
        vtb_ref[0, h, HEAD_DIM:VT_ROWS, :] = ones


def _inproj_prompt(x, w, nbatch, seq):
    m = x.shape[0]
    tm = ROW_TILE
    nb = seq // tm
    nblk = seq // MOBA_BLOCK
    rows = lambda i: (i, 0)
    tcols = lambda i: (i // nb, 0, i % nb)
    return pl.pallas_call(
        _inproj_prompt_kernel,
        grid=(m // tm,),
        in_specs=[pl.BlockSpec((tm, D_MODEL), rows),
                  pl.BlockSpec(w.shape, lambda i: (0, 0))],
        out_specs=[pl.BlockSpec((tm, ATTN_WIDTH), rows),
                   pl.BlockSpec((tm, ATTN_WIDTH), rows),
                   pl.BlockSpec((1, ATTN_WIDTH, tm), tcols),
                   pl.BlockSpec((1, ATTN_WIDTH, tm), tcols),
                   pl.BlockSpec((1, ATTN_WIDTH, tm), tcols),
                   pl.BlockSpec((tm, ATTN_WIDTH), rows),
                   pl.BlockSpec((1, ATTN_HEADS, VT_ROWS, tm), lambda i: (i // nb, 0, 0, i % nb)),
                   pl.BlockSpec((tm // MOBA_BLOCK, 1, ATTN_WIDTH), lambda i: (i, 0, 0))],
        out_shape=[jax.ShapeDtypeStruct((m, ATTN_WIDTH), F32),
                   jax.ShapeDtypeStruct((m, ATTN_WIDTH), F32),
                   jax.ShapeDtypeStruct((nbatch, ATTN_WIDTH, seq), F32),
                   jax.ShapeDtypeStruct((nbatch, ATTN_WIDTH, seq), F32),
                   jax.ShapeDtypeStruct((nbatch, ATTN_WIDTH, seq), BF16),
                   jax.ShapeDtypeStruct((m, ATTN_WIDTH), BF16),
                   jax.ShapeDtypeStruct((nbatch, ATTN_HEADS, VT_ROWS, seq), BF16),
                   jax.ShapeDtypeStruct((m // MOBA_BLOCK, 1, ATTN_WIDTH), F32)],
        compiler_params=pltpu.CompilerParams(dimension_semantics=("parallel",)),
        name="inproj_prompt",
    )(x, w)


def _inproj_sample_kernel(x_ref, w_ref, zc_ref, zl_ref, q_ref, k_ref, v_ref):
    xb = x_ref[...].astype(BF16)
    for c, o_ref in enumerate((zc_ref, zl_ref, q_ref, k_ref, v_ref)):
        o_ref[...] = jnp.dot(xb, w_ref[:, c * ATTN_WIDTH:(c + 1) * ATTN_WIDTH], preferred_element_type=F32)


def _inproj_sample(x, w):
    m = x.shape[0]
    out = jax.ShapeDtypeStruct((m, ATTN_WIDTH), F32)
    return pl.pallas_call(_inproj_sample_kernel, out_shape=[out] * 5, name="inproj_sample")(x, w)


def _mixer_prompt_kernel(zc_ref, zl_ref, cw_ref, cb_ref, lg_ref, lb_ref, lw_ref, lcb_ref,
                         wa_ref, ba_ref, wx_ref, bx_ref, lam_ref,
                         oab_ref, cst_ref, lst_ref, hst_ref,
                         glu_ext, xl_ext, h_sc):
    i = pl.program_id(1)
    tc = zc_ref.shape[0]

    @pl.when(i == 0)
    def _():
        glu_ext[0:CONV_HALO, :] = jnp.zeros((CONV_HALO, CONV_CH), F32)
        xl_ext[0:LRU_HALO, :] = jnp.zeros((LRU_HALO, LRU_CH), F32)
        h_sc[...] = jnp.zeros(h_sc.shape, F32)

    glu_ext[CONV_HALO:CONV_HALO + tc, :] = zc_ref[:, 0:CONV_CH] * _sigmoid(zc_ref[:, CONV_CH:2 * CONV_CH])
    first = CONV_HALO - (CONV_WIDTH - 1)
    for c in range(tc // CONV_CHUNK):
        acc = jnp.broadcast_to(cb_ref[...], (CONV_CHUNK, CONV_CH))
        for j in range(CONV_WIDTH):
            acc = acc + cw_ref[j:j + 1, :] * glu_ext[pl.ds(c * CONV_CHUNK + first + j, CONV_CHUNK), :]
        ln = _layer_norm(acc, lg_ref[...], lb_ref[...])
        oab_ref[c * CONV_CHUNK:(c + 1) * CONV_CHUNK, 0:CONV_CH] = ln * _sigmoid(ln)

    xl_ext[LRU_HALO:LRU_HALO + tc, :] = zl_ref[:, LRU_CH:2 * LRU_CH]
    lfirst = LRU_HALO - (LRU_CONV_WIDTH - 1)
    xr = jnp.broadcast_to(lcb_ref[...], (tc, LRU_CH))
    for j in range(LRU_CONV_WIDTH):
        xr = xr + lw_ref[j:j + 1, :] * xl_ext[pl.ds(lfirst + j, tc), :]
    a, u = _lru_coeffs(xr, wa_ref, ba_ref, wx_ref, bx_ref, lam_ref)
    row = lax.broadcasted_iota(I32, (tc, LRU_CH), 0)
    s = 1
    while s < tc:
        keep = row >= s
        u = jnp.where(keep, a * pltpu.roll(u, s, 0) + u, u)
        a = jnp.where(keep, a * pltpu.roll(a, s, 0), a)
        s *= 2
    h = a * h_sc[0:1, :] + u
    oab_ref[:, CONV_CH:CONV_CH + LRU_CH] = _gelu_tanh(zl_ref[:, 0:LRU_CH]) * h
    h_last = h[tc - 1:tc, :]

    @pl.when(i == pl.num_programs(1) - 1)
    def _():
        cst_ref[0] = glu_ext[pl.ds(tc + first, CONV_WIDTH - 1), :]
        lst_ref[0] = xl_ext[pl.ds(tc + lfirst, LRU_CONV_WIDTH - 1), :]
        hst_ref[0] = h_last

    glu_ext[0:CONV_HALO, :] = glu_ext[tc:tc + CONV_HALO, :]
    xl_ext[0:LRU_HALO, :] = xl_ext[tc:tc + LRU_HALO, :]
    h_sc[...] = jnp.broadcast_to(h_last, h_sc.shape)


def _mixer_prompt(zc, zl, p, nbatch, seq):
    tc = ROW_TILE
    nt = seq // tc
    rows = lambda b, i: (b * nt + i, 0)
    full = lambda a: pl.BlockSpec(a.shape, lambda b, i: (0,) * a.ndim)
    params = [p["conv_dw_w"], p["conv_dw_b"], p["conv_ln_g"], p["conv_ln_b"], p["lru_conv_w"], p["lru_conv_b"],
              p["wa"], p["lru_ba"], p["wx"], p["lru_bx"], p["lru_lambda"]]
    return pl.pallas_call(
        _mixer_prompt_kernel,
        grid=(nbatch, nt),
        in_specs=[pl.BlockSpec((tc, 2 * CONV_CH), rows), pl.BlockSpec((tc, 2 * LRU_CH), rows)]
        + [full(a) for a in params],
        out_specs=[pl.BlockSpec((tc, CONV_CH + LRU_CH), rows),
                   pl.BlockSpec((1, CONV_WIDTH - 1, CONV_CH), lambda b, i: (b, 0, 0)),
                   pl.BlockSpec((1, LRU_CONV_WIDTH - 1, LRU_CH), lambda b, i: (b, 0, 0)),
                   pl.BlockSpec((1, 1, LRU_CH), lambda b, i: (b, 0, 0))],
        out_shape=[jax.ShapeDtypeStruct((nbatch * seq, CONV_CH + LRU_CH), F32),
                   jax.ShapeDtypeStruct((nbatch, CONV_WIDTH - 1, CONV_CH), F32),
                   jax.ShapeDtypeStruct((nbatch, LRU_CONV_WIDTH - 1, LRU_CH), F32),
                   jax.ShapeDtypeStruct((nbatch, 1, LRU_CH), F32)],
        scratch_shapes=[pltpu.VMEM((tc + CONV_HALO, CONV_CH), F32),
                        pltpu.VMEM((tc + LRU_HALO, LRU_CH), F32),
                        pltpu.VMEM((SUBLANES, LRU_CH), F32)],
        compiler_params=pltpu.CompilerParams(dimension_semantics=("parallel", "arbitrary")),
        name="mixer_prompt",
    )(zc, zl, *params)


def _mixer_sample_kernel(zc_ref, zl_ref, cs_ref, ls_ref, h0_ref, cw_ref, cb_ref, lg_ref, lb_ref, lw_ref, lcb_ref,
                         wa_ref, ba_ref, wx_ref, bx_ref, lam_ref,
                         oab_ref, cst_ref, lst_ref, hst_ref):
    glu = zc_ref[:, 0:CONV_CH] * _sigmoid(zc_ref[:, CONV_CH:2 * CONV_CH])
    acc = cb_ref[...] + cw_ref[CONV_WIDTH - 1:CONV_WIDTH, :] * glu
    for j in range(CONV_WIDTH - 1):
        acc = acc + cw_ref[j:j + 1, :] * cs_ref[j]
        if j >= 1:
            cst_ref[j - 1] = cs_ref[j]
    cst_ref[CONV_WIDTH - 2] = glu
    ln = _layer_norm(acc, lg_ref[...], lb_ref[...])
    oab_ref[:, 0:CONV_CH] = ln * _sigmoid(ln)

    xl = zl_ref[:, LRU_CH:2 * LRU_CH]
    xr = lcb_ref[...] + lw_ref[LRU_CONV_WIDTH - 1:LRU_CONV_WIDTH, :] * xl
    for j in range(LRU_CONV_WIDTH - 1):
        xr = xr + lw_ref[j:j + 1, :] * ls_ref[j]
        if j >= 1:
            lst_ref[j - 1] = ls_ref[j]
    lst_ref[LRU_CONV_WIDTH - 2] = xl
    a, u = _lru_coeffs(xr, wa_ref, ba_ref, wx_ref, bx_ref, lam_ref)
    h = a * h0_ref[...] + u
    hst_ref[...] = h
    oab_ref[:, CONV_CH:CONV_CH + LRU_CH] = _gelu_tanh(zl_ref[:, 0:LRU_CH]) * h


def _mixer_sample(zc, zl, conv_state, lru_state, h0, p):
    ns = zc.shape[0]
    params = [p["conv_dw_w"], p["conv_dw_b"], p["conv_ln_g"], p["conv_ln_b"], p["lru_conv_w"], p["lru_conv_b"],
              p["wa"], p["lru_ba"], p["wx"], p["lru_bx"], p["lru_lambda"]]
    return pl.pallas_call(
        _mixer_sample_kernel,
        out_shape=[jax.ShapeDtypeStruct((ns, CONV_CH + LRU_CH), F32),
                   jax.ShapeDtypeStruct(conv_state.shape, F32),
                   jax.ShapeDtypeStruct(lru_state.shape, F32),
                   jax.ShapeDtypeStruct((ns, LRU_CH), F32)],
        name="mixer_sample",
    )(zc, zl, conv_state, lru_state, h0, *params)


def _select_prompt_kernel(rb_ref, km_ref, qt_ref, madd_ref):
    i = pl.program_id(1)
    nbp = madd_ref.shape[2] - SUBLANES
    gate = jnp.dot(km_ref[0], qt_ref[0].astype(F32), preferred_element_type=F32,
                   precision=lax.Precision.HIGHEST)
    n = lax.broadcasted_iota(I32, (nbp, MOBA_BLOCK), 0)
    nf = n.astype(F32)
    past = n < i
    tail_row = lax.broadcasted_iota(I32, (SUBLANES, MOBA_BLOCK), 0)
    for h in range(ATTN_HEADS):
        g = jnp.where(past, gate[h * nbp:(h + 1) * nbp], NEG_INF)
        sel = jnp.zeros((nbp, MOBA_BLOCK), jnp.bool_)
        for _ in range(MOBA_TOPK):
            best = jnp.max(g, axis=0, keepdims=True)
            idx = jnp.min(jnp.where(g == best, nf, float(nbp)), axis=0, keepdims=True)
            pick = nf == idx
            sel = jnp.logical_or(sel, pick)
            g = jnp.where(pick, -jnp.inf, g)
        far_bias = rb_ref[REL_BUCKETS - 1, h] * LOG2E
        madd_ref[0, h, 0:nbp, :] = jnp.where(jnp.logical_and(sel, n < i - 1), far_bias, NEG_INF)
        prev = jnp.max(jnp.where(jnp.logical_and(sel, n == i - 1), 0.0, NEG_INF), axis=0, keepdims=True)
        madd_ref[0, h, nbp:nbp + SUBLANES, :] = jnp.where(tail_row == 0, prev, NEG_INF)


def _select_prompt(rel_bias, kmbd, qtb, nbp):
    nbatch, _, seq = qtb.shape
    nblk = seq // MOBA_BLOCK
    rows = nbp + SUBLANES
    return pl.pallas_call(
        _select_prompt_kernel,
        grid=(nbatch, nblk),
        in_specs=[pl.BlockSpec(memory_space=pltpu.SMEM),
                  pl.BlockSpec((1, ATTN_HEADS * nbp, ATTN_WIDTH), lambda b, i: (b, 0, 0)),
                  pl.BlockSpec((1, ATTN_WIDTH, MOBA_BLOCK), lambda b, i: (b, 0, i))],
        out_specs=pl.BlockSpec((1, ATTN_HEADS, rows, MOBA_BLOCK), lambda b, i: (b, 0, 0, i)),
        out_shape=jax.ShapeDtypeStruct((nbatch, ATTN_HEADS, rows, seq), F32),
        compiler_params=pltpu.CompilerParams(dimension_semantics=("parallel", "arbitrary")),
        name="select_prompt",
    )(rel_bias, kmbd, qtb)


def _far_schedule(nblk):
    qblk, base, first = [], [], []
    for i in range(nblk):
        for g in range((i + FAR_GROUP - 2) // FAR_GROUP):
            qblk.append(i)
            base.append(g * FAR_GROUP)
            first.append(1 if g == 0 else 0)
    return np.array([qblk, base, first], np.int32)


def _attn_prompt_kernel(sched_ref, qt_ref, k_ref, vt_ref, madd_ref, t0_ref, t1_ref, o_ref,
                        sn_sc, sf_sc, p_sc, mn_sc, accn_sc, acc_sc):
    nblk = qt_ref.shape[2] // MOBA_BLOCK
    nbp = madd_ref.shape[2] - SUBLANES
    ntile = sched_ref.shape[1]
    width = 2 * MOBA_BLOCK
    near = 2 * MOBA_BLOCK
    group = FAR_GROUP * MOBA_BLOCK

    def qcols(i):
        return pl.ds(pl.multiple_of(i * MOBA_BLOCK, MOBA_BLOCK), MOBA_BLOCK)

    def both_heads(i):
        qt = qt_ref[0, :, qcols(i)]
        z = jnp.zeros((HEAD_DIM, MOBA_BLOCK), BF16)
        return jnp.concatenate([jnp.concatenate([qt[0:HEAD_DIM], z], axis=1),
                                jnp.concatenate([z, qt[HEAD_DIM:2 * HEAD_DIM]], axis=1)], axis=0)

    def lanes2(fn):
        return jnp.concatenate([fn(0), fn(1)], axis=1)

    def pv(rows0, nrows):
        return jnp.stack([jnp.dot(vt_ref[0, e, :, pl.ds(rows0, nrows)],
                                  p_sc[0:nrows, e * MOBA_BLOCK:(e + 1) * MOBA_BLOCK],
                                  preferred_element_type=F32) for e in range(2)])

    def write_out(i, acc):
        ot = jnp.concatenate([acc[e, 0:HEAD_DIM] / acc[e, HEAD_DIM:HEAD_DIM + 1] for e in range(2)], axis=0)
        o_ref[pl.ds(pl.multiple_of(i * MOBA_BLOCK, MOBA_BLOCK), MOBA_BLOCK), :] = ot.T

    def near_x(i, slot, first):
        r0 = 0 if first else pl.multiple_of((i - 1) * MOBA_BLOCK, MOBA_BLOCK)
        qb = both_heads(i)
        cmax = None
        for c in range(2):
            s = jnp.dot(k_ref[pl.ds(r0 + c * MOBA_BLOCK, MOBA_BLOCK), :], qb, preferred_element_type=F32)
            if first:
                s = s + (lanes2(lambda e: t0_ref[e]) if c == 0 else NEG_INF)
            elif c == 0:
                s = s + (lanes2(lambda e: t1_ref[e]) + lanes2(lambda e: madd_ref[0, e, nbp:nbp + 1, qcols(i)]))
            else:
                s = s + lanes2(lambda e: t0_ref[e])
            sn_sc[slot, c * MOBA_BLOCK:(c + 1) * MOBA_BLOCK, :] = s
            bmax = jnp.max(s, axis=0, keepdims=True)
            cmax = bmax if cmax is None else jnp.maximum(cmax, bmax)
        return cmax

    def near_y(i, slot, m, first):
        r0 = 0 if first else pl.multiple_of((i - 1) * MOBA_BLOCK, MOBA_BLOCK)
        for c in range(2):
            rows = slice(c * MOBA_BLOCK, (c + 1) * MOBA_BLOCK)
            p_sc[rows, :] = jnp.exp2(sn_sc[slot, rows, :] - m).astype(BF16)
        acc = pv(r0, near)
        mn_sc[pl.ds(i, 1), :] = m
        accn_sc[i] = acc
        write_out(i, acc)

    m0 = near_x(0, 0, True)
    m1 = near_x(1, 1, False)
    near_y(0, 0, m0, True)

    def near_body(h, m_odd):
        i = 2 * h
        m_even = near_x(i, 0, False)
        near_y(i - 1, 1, m_odd, False)
        m_odd = near_x(i + 1, 1, False)
        near_y(i, 0, m_even, False)
        return m_odd

    m_last = lax.fori_loop(1, nblk // 2, near_body, m1)
    near_y(nblk - 1, 1, m_last, False)

    def tile(n):
        return sched_ref[0, n], sched_ref[1, n], sched_ref[2, n]

    def far_rows(i, base):
        return [lanes2(lambda e: madd_ref[0, e, pl.ds(base + c, 1), qcols(i)]) for c in range(FAR_GROUP)]

    def far_x(n, slot):
        i, base, _ = tile(n)
        qb = both_heads(i)

        def block(c):
            kc = k_ref[pl.ds(pl.multiple_of((base + c) * MOBA_BLOCK, MOBA_BLOCK), MOBA_BLOCK), :]
            s = jnp.dot(kc, qb, preferred_element_type=F32)
            sf_sc[slot, c * MOBA_BLOCK:(c + 1) * MOBA_BLOCK, :] = s
            return jnp.max(s, axis=0, keepdims=True)
        return block

    def far_state(n, bmax, m_prev):
        i, base, first = tile(n)
        m_in = jnp.where(first == 1, mn_sc[pl.ds(i, 1), :], m_prev)
        add = far_rows(i, base)
        m_new = m_in
        for c in range(FAR_GROUP):
            m_new = jnp.maximum(m_new, bmax[c] + add[c])
        return m_in, m_new

    def far_y(n, slot, m_in, m_new, x_next=None):
        i, base, first = tile(n)
        add = far_rows(i, base)
        alpha = jnp.exp2(m_in - m_new)
        nxt = []
        for c in range(FAR_GROUP):
            if x_next is not None:
                nxt.append(x_next(c))
            rows = slice(c * MOBA_BLOCK, (c + 1) * MOBA_BLOCK)
            p_sc[rows, :] = jnp.exp2(sf_sc[slot, rows, :] - (m_new - add[c])).astype(BF16)
        acc_in = jnp.where(first == 1, accn_sc[i], acc_sc[...])
        scale = jnp.stack([jnp.broadcast_to(alpha[:, e * MOBA_BLOCK:(e + 1) * MOBA_BLOCK], (VT_ROWS, MOBA_BLOCK))
                           for e in range(2)])
        acc = scale * acc_in + pv(pl.multiple_of(base * MOBA_BLOCK, group), group)
        acc_sc[...] = acc
        write_out(i, acc)
        return nxt

    if ntile:
        assert ntile % 2 == 0
        first_block = far_x(0, 0)
        carry0 = far_state(0, [first_block(c) for c in range(FAR_GROUP)], jnp.zeros((1, width), F32))

        def step(n, slot, carry):
            m_in, m_new = carry
            bmax = far_y(n, slot, m_in, m_new, far_x(n + 1, 1 - slot))
            return far_state(n + 1, bmax, m_new)

        def far_body(h, carry):
            return step(2 * h + 1, 1, step(2 * h, 0, carry))

        carry = lax.fori_loop(0, ntile // 2 - 1, far_body, carry0)
        m_in, m_new = step(ntile - 2, 0, carry)
        far_y(ntile - 1, 1, m_in, m_new)


def _attn_prompt(qtb, kb, vtb, madd, t0, t1):
    nbatch, _, seq = qtb.shape
    nblk = seq // MOBA_BLOCK
    rows = madd.shape[2]
    width = 2 * MOBA_BLOCK
    sched = _far_schedule(nblk)
    return pl.pallas_call(
        _attn_prompt_kernel,
        grid_spec=pltpu.PrefetchScalarGridSpec(
            num_scalar_prefetch=1,
            grid=(nbatch, HEAD_PAIRS),
            in_specs=[pl.BlockSpec((1, 2 * HEAD_DIM, seq), lambda b, p, s: (b, p, 0)),
                      pl.BlockSpec((seq, 2 * HEAD_DIM), lambda b, p, s: (b, p)),
                      pl.BlockSpec((1, 2, VT_ROWS, seq), lambda b, p, s: (b, p, 0, 0)),
                      pl.BlockSpec((1, 2, rows, seq), lambda b, p, s: (b, p, 0, 0)),
                      pl.BlockSpec((2, MOBA_BLOCK, MOBA_BLOCK), lambda b, p, s: (p, 0, 0)),
                      pl.BlockSpec((2, MOBA_BLOCK, MOBA_BLOCK), lambda b, p, s: (p, 0, 0))],
            out_specs=pl.BlockSpec((seq, 2 * HEAD_DIM), lambda b, p, s: (b, p)),
            scratch_shapes=[pltpu.VMEM((2, 2 * MOBA_BLOCK, width), F32),
                            pltpu.VMEM((2, FAR_GROUP * MOBA_BLOCK, width), F32),
                            pltpu.VMEM((FAR_GROUP * MOBA_BLOCK, width), BF16),
                            pltpu.VMEM((nblk, width), F32),
                            pltpu.VMEM((nblk, 2, VT_ROWS, MOBA_BLOCK), F32),
                            pltpu.VMEM((2, VT_ROWS, MOBA_BLOCK), F32)]),
        out_shape=jax.ShapeDtypeStruct((nbatch * seq, ATTN_WIDTH), F32),
        compiler_params=pltpu.CompilerParams(dimension_semantics=("parallel", "parallel")),
        name="attn_prompt",
    )(jnp.asarray(sched), qtb, kb, vtb, madd, t0, t1)


def _make_kv_select_kernel(layer, n_pages):
    nchunk = n_pages // KV_CHUNK_PAGES
    blocks_per_chunk = KV_CHUNK_PAGES // PAGES_PER_BLOCK

    def page_copy(pt_ref, cache_ref, buf, sem, b, c, slot, p):
        page = pt_ref[b, c * KV_CHUNK_PAGES + p]
        return pltpu.make_async_copy(cache_ref.at[layer, page], buf.at[slot, p], sem.at[slot])

    def kernel(pt_ref, cache_ref, q_ref, sel_ref, buf, sem, kmt):
        b = pl.program_id(0)
        c = pl.program_id(1)
        step = b * nchunk + c
        slot = step % 2

        @pl.when(step == 0)
        def _():
            for p in range(KV_CHUNK_PAGES):
                page_copy(pt_ref, cache_ref, buf, sem, 0, 0, 0, p).start()

        @pl.when(step + 1 < pl.num_programs(0) * nchunk)
        def _():
            nxt = step + 1
            for p in range(KV_CHUNK_PAGES):
                page_copy(pt_ref, cache_ref, buf, sem, nxt // nchunk, nxt % nchunk, 1 - slot, p).start()

        for p in range(KV_CHUNK_PAGES):
            page_copy(pt_ref, cache_ref, buf, sem, b, c, slot, p).wait()

        @pl.when(c == 0)
        def _():
            kmt[...] = jnp.zeros(kmt.shape, F32)

        lane = lax.broadcasted_iota(I32, kmt.shape, 1)
        km = kmt[...]
        for kb in range(blocks_per_chunk):
            tot = buf[slot, PAGES_PER_BLOCK * kb]
            for r in range(1, PAGES_PER_BLOCK):
                tot = tot + buf[slot, PAGES_PER_BLOCK * kb + r]
            mean = jnp.sum(tot.reshape(ATTN_WIDTH, PAGE_SIZE), axis=1, keepdims=True) * (1.0 / MOBA_BLOCK)
            km = jnp.where(lane == c * blocks_per_chunk + kb, mean, km)
        kmt[...] = km

        @pl.when(c == nchunk - 1)
        def _():
            nblk = n_pages // PAGES_PER_BLOCK
            hrow = lax.broadcasted_iota(I32, (SUBLANES, ATTN_WIDTH), 0)
            hcol = lax.broadcasted_iota(I32, (SUBLANES, ATTN_WIDTH), 1) // HEAD_DIM
            qbd = jnp.where(hrow == hcol, jnp.broadcast_to(q_ref[0], (SUBLANES, ATTN_WIDTH)), 0.0)
            gate = jnp.dot(qbd, km, preferred_element_type=F32, precision=lax.Precision.HIGHEST)
            nf = lax.broadcasted_iota(I32, gate.shape, 1).astype(F32)
            g = jnp.where(nf < nblk, gate, -jnp.inf)
            for r in range(MOBA_TOPK):
                best = jnp.max(g, axis=1, keepdims=True)
                idx = jnp.min(jnp.where(g == best, nf, float(LANES)), axis=1, keepdims=True)
                sel_ref[0, r] = jnp.broadcast_to(idx, (SUBLANES, LANES)).astype(I32)
                g = jnp.where(nf == idx, -jnp.inf, g)

    return kernel


def _kv_select(page_table, cache_t, q3, layer):
    ns, n_pages = page_table.shape
    nchunk = n_pages // KV_CHUNK_PAGES
    return pl.pallas_call(
        _make_kv_select_kernel(layer, n_pages),
        grid_spec=pltpu.PrefetchScalarGridSpec(
            num_scalar_prefetch=1,
            grid=(ns, nchunk),
            in_specs=[pl.BlockSpec(memory_space=pl.ANY),
                      pl.BlockSpec((1, 1, ATTN_WIDTH), lambda b, c, pt: (b, 0, 0))],
            out_specs=pl.BlockSpec((1, MOBA_TOPK, SUBLANES, LANES), lambda b, c, pt: (b, 0, 0, 0)),
            scratch_shapes=[pltpu.VMEM((2, KV_CHUNK_PAGES, ATTN_HEADS, HEAD_DIM, PAGE_SIZE), F32),
                            pltpu.SemaphoreType.DMA((2,)),
                            pltpu.VMEM((ATTN_WIDTH, LANES), F32)]),
        out_shape=jax.ShapeDtypeStruct((ns, MOBA_TOPK, SUBLANES, LANES), I32),
        compiler_params=pltpu.CompilerParams(dimension_semantics=("arbitrary", "arbitrary")),
        name="kv_select",
    )(page_table, cache_t, q3)


def _make_attn_sample_kernel(layer, n_pages):
    nblk = n_pages // PAGES_PER_BLOCK
    nsel = MOBA_TOPK * MOBA_BLOCK

    def copies(pt_ref, sel_ref, ck_ref, cv_ref, kbuf, vbuf, sem, b, slot):
        out = []
        for h in range(ATTN_HEADS):
            for r in range(MOBA_TOPK):
                blk = sel_ref[b, h * MOBA_TOPK + r]
                for half in range(PAGES_PER_BLOCK):
                    page = pt_ref[b, blk * PAGES_PER_BLOCK + half]
                    dst = pl.ds(r * MOBA_BLOCK + half * PAGE_SIZE, PAGE_SIZE)
                    out.append(pltpu.make_async_copy(ck_ref.at[layer, page, h], kbuf.at[slot, h, :, dst], sem.at[0, slot]))
                    out.append(pltpu.make_async_copy(cv_ref.at[layer, page, h], vbuf.at[slot, h, :, dst], sem.at[1, slot]))
        return out

    def kernel(pt_ref, sel_ref, rb_ref, ck_ref, cv_ref, q_ref, kn_ref, vn_ref, bs_ref, o_ref, kbuf, vbuf, sem):
        b = pl.program_id(0)
        slot = b % 2

        @pl.when(b == 0)
        def _():
            for cp in copies(pt_ref, sel_ref, ck_ref, cv_ref, kbuf, vbuf, sem, 0, 0):
                cp.start()

        @pl.when(b + 1 < pl.num_programs(0))
        def _():
            for cp in copies(pt_ref, sel_ref, ck_ref, cv_ref, kbuf, vbuf, sem, b + 1, 1 - slot):
                cp.start()

        for cp in copies(pt_ref, sel_ref, ck_ref, cv_ref, kbuf, vbuf, sem, b, slot):
            cp.wait()

        q = q_ref[0] * Q_SCALE
        kn = kn_ref[0]
        vn = vn_ref[0]
        for h in range(ATTN_HEADS):
            hs = slice(h * HEAD_DIM, (h + 1) * HEAD_DIM)
            qh = q[:, hs]
            q8 = jnp.broadcast_to(qh, (SUBLANES, HEAD_DIM)).astype(BF16)
            s = jnp.dot(q8, kbuf[slot, h].astype(BF16), preferred_element_type=F32)
            far_bias = rb_ref[REL_BUCKETS - 1, h]
            bias = jnp.concatenate(
                [jnp.where(sel_ref[b, h * MOBA_TOPK + r] == nblk - 1, bs_ref[h], far_bias)
                 for r in range(MOBA_TOPK)], axis=1)
            s = s + bias
            s_new = jnp.sum(qh * kn[:, hs], axis=1, keepdims=True) + rb_ref[0, h]
            m = jnp.maximum(jnp.max(s, axis=1, keepdims=True), s_new)
            p = jnp.exp(s - m)
            p_new = jnp.exp(s_new - m)
            l = jnp.sum(p, axis=1, keepdims=True) + p_new
            o = lax.dot_general(p.astype(BF16), vbuf[slot, h].astype(BF16), (((1,), (1,)), ((), ())),
                                preferred_element_type=F32)
            o = (o + p_new * vn[:, hs]) / l
            o_ref[0, :, hs] = o[0:1]

    return kernel


def _attn_sample(page_table, sel, rel_bias, cache_kt, cache_vt, q3, k3, v3, bs, layer):
    ns, n_pages = page_table.shape
    nsel = MOBA_TOPK * MOBA_BLOCK
    row = pl.BlockSpec((1, 1, ATTN_WIDTH), lambda b, pt, sl: (b, 0, 0))
    return pl.pallas_call(
        _make_attn_sample_kernel(layer, n_pages),
        grid_spec=pltpu.PrefetchScalarGridSpec(
            num_scalar_prefetch=2,
            grid=(ns,),
            in_specs=[pl.BlockSpec(memory_space=pltpu.SMEM),
                      pl.BlockSpec(memory_space=pl.ANY),
                      pl.BlockSpec(memory_space=pl.ANY),
                      row, row, row,
                      pl.BlockSpec(bs.shape, lambda b, pt, sl: (0, 0, 0))],
            out_specs=row,
            scratch_shapes=[pltpu.VMEM((2, ATTN_HEADS, HEAD_DIM, nsel), F32),
                            pltpu.VMEM((2, ATTN_HEADS, HEAD_DIM, nsel), F32),
                            pltpu.SemaphoreType.DMA((2, 2))]),
        out_shape=jax.ShapeDtypeStruct((ns, 1, ATTN_WIDTH), F32),
        compiler_params=pltpu.CompilerParams(dimension_semantics=("arbitrary",)),
        name="attn_sample",
    )(page_table, sel, rel_bias, cache_kt, cache_vt, q3, k3, v3, bs)


def _outproj_kernel(ab_ref, c_ref, x_ref, w_ref, g_ref, b_ref, o_ref):
    split = CONV_CH + LRU_CH
    y = jnp.dot(ab_ref[...].astype(BF16), w_ref[0:split, :], preferred_element_type=F32)
    y = y + jnp.dot(c_ref[...].astype(BF16), w_ref[split:split + ATTN_WIDTH, :], preferred_element_type=F32)
    o_ref[...] = _layer_norm(DEEPNORM_ALPHA * x_ref[...] + y, g_ref[...], b_ref[...])


def _outproj(oab, oc, x, w, g, b):
    m = x.shape[0]
    tm = min(ROW_TILE, m)
    rows = lambda i: (i, 0)
    const = lambda a: pl.BlockSpec(a.shape, lambda i: (0, 0))
    return pl.pallas_call(
        _outproj_kernel,
        grid=(m // tm,),
        in_specs=[pl.BlockSpec((tm, CONV_CH + LRU_CH), rows), pl.BlockSpec((tm, ATTN_WIDTH), rows),
                  pl.BlockSpec((tm, D_MODEL), rows), const(w), const(g), const(b)],
        out_specs=pl.BlockSpec((tm, D_MODEL), rows),
        out_shape=jax.ShapeDtypeStruct((m, D_MODEL), F32),
        compiler_params=pltpu.CompilerParams(dimension_semantics=("parallel",)),
        name="outproj",
    )(oab, oc, x, w, g, b)


def _make_ffn_prompt_kernel(tiles_per_seq):
    halo = BF16_ROWS
    taps = FFN_CONV_WIDTH

    def kernel(x_ref, xh_ref, wu_ref, wg_ref, dw_ref, db_ref, wd_ref, g_ref, b_ref, o_ref, fst_ref, xb_sc, up_sc, acc_sc):
        i = pl.program_id(0)
        n = pl.program_id(1)
        tm = x_ref.shape[0]

        @pl.when(n == 0)
        def _():
            keep = jnp.where(i % tiles_per_seq == 0, 0.0, 1.0)
            xb_sc[0:halo, :] = (xh_ref[...] * keep).astype(BF16)
            xb_sc[halo:halo + tm, :] = x_ref[...].astype(BF16)
            acc_sc[...] = jnp.zeros(acc_sc.shape, F32)

        up_sc[...] = jnp.dot(xb_sc[...], wu_ref[...], preferred_element_type=F32)
        y = jnp.broadcast_to(db_ref[...], (tm, up_sc.shape[1]))
        for j in range(taps):
            y = y + dw_ref[j:j + 1, :] * up_sc[pl.ds(halo - (taps - 1) + j, tm), :]
        fst_ref[0] = up_sc[pl.ds(halo + tm - (taps - 1), taps - 1), :]
        gate = jnp.dot(xb_sc[halo:halo + tm, :], wg_ref[...], preferred_element_type=F32)
        acc_sc[...] += jnp.dot((_gelu_tanh(y) * gate).astype(BF16), wd_ref[...], preferred_element_type=F32)

        @pl.when(n == pl.num_programs(1) - 1)
        def _():
            o_ref[...] = _layer_norm(DEEPNORM_ALPHA * x_ref[...] + acc_sc[...], g_ref[...], b_ref[...])

    return kernel


def _ffn_prompt(x, p, nbatch, seq):
    m = x.shape[0]
    tm, tn = ROW_TILE, FF_TILE
    nt = seq // tm
    halo = BF16_ROWS
    hb = tm // halo
    y, tails = pl.pallas_call(
        _make_ffn_prompt_kernel(nt),
        grid=(m // tm, D_FF // tn),
        in_specs=[pl.BlockSpec((tm, D_MODEL), lambda i, n: (i, 0)),
                  pl.BlockSpec((halo, D_MODEL), lambda i, n: (jnp.maximum(i * hb - 1, 0), 0)),
                  pl.BlockSpec((D_MODEL, tn), lambda i, n: (0, n)),
                  pl.BlockSpec((D_MODEL, tn), lambda i, n: (0, n)),
                  pl.BlockSpec((FFN_CONV_WIDTH, tn), lambda i, n: (0, n)),
                  pl.BlockSpec((1, tn), lambda i, n: (0, n)),
                  pl.BlockSpec((tn, D_MODEL), lambda i, n: (n, 0)),
                  pl.BlockSpec((1, D_MODEL), lambda i, n: (0, 0)),
                  pl.BlockSpec((1, D_MODEL), lambda i, n: (0, 0))],
        out_specs=[pl.BlockSpec((tm, D_MODEL), lambda i, n: (i, 0)),
                   pl.BlockSpec((1, FFN_CONV_WIDTH - 1, tn), lambda i, n: (i, 0, n))],
        out_shape=[jax.ShapeDtypeStruct((m, D_MODEL), F32),
                   jax.ShapeDtypeStruct((m // tm, FFN_CONV_WIDTH - 1, D_FF), F32)],
        scratch_shapes=[pltpu.VMEM((tm + halo, D_MODEL), BF16),
                        pltpu.VMEM((tm + halo, tn), F32),
                        pltpu.VMEM((tm, D_MODEL), F32)],
        compiler_params=pltpu.CompilerParams(dimension_semantics=("arbitrary", "arbitrary")),
        name="ffn_prompt",
    )(x, x, p["w_up"], p["w_gate"], p["ffn_dw_w"], p["ffn_dw_b"], p["w_down"], p["ln2_g"], p["ln2_b"])
    return y, tails[nt - 1::nt]


def _ffn_sample_kernel(x_ref, st_ref, wu_ref, wg_ref, dw_ref, db_ref, wd_ref, g_ref, b_ref, o_ref, fst_ref, acc_sc):
    n = pl.program_id(0)
    taps = FFN_CONV_WIDTH

    @pl.when(n == 0)
    def _():
        acc_sc[...] = jnp.zeros(acc_sc.shape, F32)

    xb = x_ref[...].astype(BF16)
    up = jnp.dot(xb, wu_ref[...], preferred_element_type=F32)
    y = db_ref[...] + dw_ref[taps - 1:taps, :] * up
    for j in range(taps - 1):
        y = y + dw_ref[j:j + 1, :] * st_ref[j]
        if j >= 1:
            fst_ref[j - 1] = st_ref[j]
    fst_ref[taps - 2] = up
    gate = jnp.dot(xb, wg_ref[...], preferred_element_type=F32)
    acc_sc[...] += jnp.dot((_gelu_tanh(y) * gate).astype(BF16), wd_ref[...], preferred_element_type=F32)

    @pl.when(n == pl.num_programs(0) - 1)
    def _():
        o_ref[...] = _layer_norm(DEEPNORM_ALPHA * x_ref[...] + acc_sc[...], g_ref[...], b_ref[...])


def _ffn_sample(x, state, p):
    ns = x.shape[0]
    tn = FF_TILE
    return pl.pallas_call(
        _ffn_sample_kernel,
        grid=(D_FF // tn,),
        in_specs=[pl.BlockSpec((ns, D_MODEL), lambda n: (0, 0)),
                  pl.BlockSpec((FFN_CONV_WIDTH - 1, ns, tn), lambda n: (0, 0, n)),
                  pl.BlockSpec((D_MODEL, tn), lambda n: (0, n)),
                  pl.BlockSpec((D_MODEL, tn), lambda n: (0, n)),
                  pl.BlockSpec((FFN_CONV_WIDTH, tn), lambda n: (0, n)),
                  pl.BlockSpec((1, tn), lambda n: (0, n)),
                  pl.BlockSpec((tn, D_MODEL), lambda n: (n, 0)),
                  pl.BlockSpec((1, D_MODEL), lambda n: (0, 0)),
                  pl.BlockSpec((1, D_MODEL), lambda n: (0, 0))],
        out_specs=[pl.BlockSpec((ns, D_MODEL), lambda n: (0, 0)),
                   pl.BlockSpec((FFN_CONV_WIDTH - 1, ns, tn), lambda n: (0, 0, n))],
        out_shape=[jax.ShapeDtypeStruct((ns, D_MODEL), F32),
                   jax.ShapeDtypeStruct((FFN_CONV_WIDTH - 1, ns, D_FF), F32)],
        scratch_shapes=[pltpu.VMEM((ns, D_MODEL), F32)],
        compiler_params=pltpu.CompilerParams(dimension_semantics=("arbitrary",)),
        name="ffn_sample",
    )(x, state, p["w_up"], p["w_gate"], p["ffn_dw_w"], p["ffn_dw_b"], p["w_down"], p["ln2_g"], p["ln2_b"])


def _block_diag(w):
    nb, n, _ = w.shape
    eye = jnp.eye(nb, dtype=w.dtype)
    return (eye[:, None, :, None] * w[:, :, None, :]).reshape(nb * n, nb * n)


def _layer_params(l, w_in, conv_dw_w, conv_dw_b, conv_ln_g, conv_ln_b, lru_conv_w, lru_conv_b, lru_wa, lru_ba,
                  lru_wx, lru_bx, lru_lambda, w_out, ln1_g, ln1_b, ffn_w_up, ffn_w_gate, ffn_dw_w, ffn_dw_b,
                  ffn_w_down, ln2_g, ln2_b):
    row = lambda a: a[l][None, :]
    return {
        "w_in": w_in[l].astype(BF16), "conv_dw_w": conv_dw_w[l], "conv_dw_b": row(conv_dw_b),
        "conv_ln_g": row(conv_ln_g), "conv_ln_b": row(conv_ln_b), "lru_conv_w": lru_conv_w[l],
        "lru_conv_b": row(lru_conv_b), "wa": _block_diag(lru_wa[l]).astype(BF16), "lru_ba": row(lru_ba),
        "wx": _block_diag(lru_wx[l]).astype(BF16), "lru_bx": row(lru_bx), "lru_lambda": row(lru_lambda),
        "w_out": w_out[l].astype(BF16), "ln1_g": row(ln1_g), "ln1_b": row(ln1_b),
        "w_up": ffn_w_up[l].astype(BF16), "w_gate": ffn_w_gate[l].astype(BF16), "ffn_dw_w": ffn_dw_w[l],
        "ffn_dw_b": row(ffn_dw_b), "w_down": ffn_w_down[l].astype(BF16), "ln2_g": row(ln2_g), "ln2_b": row(ln2_b),
    }


def _block_diag_means(kmean, nbatch, nblk, nbp):
    km = kmean.reshape(nbatch, nblk, ATTN_HEADS, HEAD_DIM).transpose(0, 2, 1, 3)
    km = jnp.pad(km, ((0, 0), (0, 0), (0, nbp - nblk), (0, 0)))
    eye = jnp.eye(ATTN_HEADS, dtype=F32)
    bd = km[:, :, :, None, :] * eye[None, :, None, :, None]
    return bd.reshape(nbatch, ATTN_HEADS * nbp, ATTN_WIDTH)


def kernel(x_prompt, x_sample, cache_k, cache_v, page_table, state_conv, state_lru_conv, state_lru_h, state_ffn_conv, w_in, conv_dw_w, conv_dw_b, conv_ln_g, conv_ln_b, lru_conv_w, lru_conv_b, lru_wa, lru_ba, lru_wx, lru_bx, lru_lambda, rel_bias, w_out, ln1_g, ln1_b, ffn_w_up, ffn_w_gate, ffn_dw_w, ffn_dw_b, ffn_w_down, ln2_g, ln2_b):
    nbatch, seq, _ = x_prompt.shape
    ns = x_sample.shape[0]
    depth = w_in.shape[0]
    nblk = seq // MOBA_BLOCK
    nbp = -(-nblk // SUBLANES) * SUBLANES
    assert seq % (FAR_GROUP * MOBA_BLOCK) == 0 and seq % ROW_TILE == 0 and x_sample.shape[1] == 1
    assert page_table.shape[1] % KV_CHUNK_PAGES == 0 and cache_k.shape[2] == PAGE_SIZE

    t0, t1, bs = _bias_tables(rel_bias)
    cache_kt = cache_k.transpose(0, 1, 3, 4, 2)
    cache_vt = cache_v.transpose(0, 1, 3, 4, 2)

    xp = x_prompt.reshape(nbatch * seq, D_MODEL)
    xs = x_sample.reshape(ns, D_MODEL)
    outs = {k: [] for k in ("k_p", "v_p", "k_s", "v_s", "conv_p", "conv_s", "lruc_p", "lruc_s",
                            "lruh_p", "lruh_s", "ffn_p", "ffn_s")}
    for l in range(depth):
        p = _layer_params(l, w_in, conv_dw_w, conv_dw_b, conv_ln_g, conv_ln_b, lru_conv_w, lru_conv_b, lru_wa,
                          lru_ba, lru_wx, lru_bx, lru_lambda, w_out, ln1_g, ln1_b, ffn_w_up, ffn_w_gate,
                          ffn_dw_w, ffn_dw_b, ffn_w_down, ln2_g, ln2_b)

        zc, zl, kt, vt, qtb, kb, vtb, kmean = _inproj_prompt(xp, p["w_in"], nbatch, seq)
        oab, cst, lst, hst = _mixer_prompt(zc, zl, p, nbatch, seq)
        madd = _select_prompt(rel_bias, _block_diag_means(kmean, nbatch, nblk, nbp), qtb, nbp)
        oc = _attn_prompt(qtb, kb, vtb, madd, t0, t1)
        x1 = _outproj(oab, oc, xp, p["w_out"], p["ln1_g"], p["ln1_b"])
        xp, fst = _ffn_prompt(x1, p, nbatch, seq)
        to_heads = lambda a: a.reshape(nbatch, ATTN_HEADS, HEAD_DIM, seq).transpose(0, 3, 1, 2)
        outs["k_p"].append(to_heads(kt))
        outs["v_p"].append(to_heads(vt))
        outs["conv_p"].append(cst)
        outs["lruc_p"].append(lst)
        outs["lruh_p"].append(hst.reshape(nbatch, LRU_CH))
        outs["ffn_p"].append(fst)

        zc, zl, q, k, v = _inproj_sample(xs, p["w_in"])
        oab, cst, lst, hst = _mixer_sample(zc, zl, state_conv[l].transpose(1, 0, 2),
                                           state_lru_conv[l].transpose(1, 0, 2), state_lru_h[l], p)
        q3, k3, v3 = (a.reshape(ns, 1, ATTN_WIDTH) for a in (q, k, v))
        sel = _kv_select(page_table, cache_kt, q3, l)
        sel = sel[:, :, :, 0].transpose(0, 2, 1).reshape(ns, ATTN_HEADS * MOBA_TOPK)
        oc = _attn_sample(page_table, sel, rel_bias, cache_kt, cache_vt, q3, k3, v3, bs, l)
        x1 = _outproj(oab, oc.reshape(ns, ATTN_WIDTH), xs, p["w_out"], p["ln1_g"], p["ln1_b"])
        xs, fst = _ffn_sample(x1, state_ffn_conv[l].transpose(1, 0, 2), p)
        outs["k_s"].append(k.reshape(ns, 1, ATTN_HEADS, HEAD_DIM))
        outs["v_s"].append(v.reshape(ns, 1, ATTN_HEADS, HEAD_DIM))
        outs["conv_s"].append(cst.transpose(1, 0, 2))
        outs["lruc_s"].append(lst.transpose(1, 0, 2))
        outs["lruh_s"].append(hst)
        outs["ffn_s"].append(fst.transpose(1, 0, 2))

    st = {k: jnp.stack(v) for k, v in outs.items()}
    return (xp.reshape(nbatch, seq, D_MODEL), xs.reshape(ns, 1, D_MODEL),
            st["k_p"], st["v_p"], st["k_s"], st["v_s"], st["conv_p"], st["conv_s"],
            st["lruc_p"], st["lruc_s"], st["lruh_p"], st["lruh_s"], st["ffn_p"], st["ffn_s"])
```

```python
import functools
import math

import jax
import jax.numpy as jnp
import numpy as np
from jax import lax
from jax.experimental import pallas as pl
from jax.experimental.pallas import tpu as pltpu

F32 = jnp.float32
BF16 = jnp.bfloat16
I32 = jnp.int32

D_MODEL = 1024
HEAD_DIM = 64
CONV_CH = 256
CONV_WIDTH = 31
LRU_CH = 256
LRU_CONV_WIDTH = 4
LRU_C = 8.0
ATTN_WIDTH = 512
ATTN_HEADS = 8
HEAD_PAIRS = ATTN_HEADS // 2
MOBA_BLOCK = 256
MOBA_TOPK = 3
REL_BUCKETS = 32
REL_MAX_DIST = 128
D_FF = 3 * D_MODEL
FFN_CONV_WIDTH = 3
PAGE_SIZE = 128
PAGES_PER_BLOCK = MOBA_BLOCK // PAGE_SIZE
DEPTH = 4
DEEPNORM_ALPHA = (2 * DEPTH) ** 0.25
LN_EPS = 1e-5
NEG_INF = -1e30
Q_SCALE = HEAD_DIM ** -0.5
LOG2E = math.log2(math.e)

LANES = 128
SUBLANES = 8
BF16_ROWS = 16
VT_ROWS = HEAD_DIM + BF16_ROWS

ROW_TILE = 512
FFN_ROW_TILE = 1024
FF_TILE = 512
CONV_CHUNK = 64
CONV_HALO = 32
LRU_HALO = 8
KV_CHUNK_PAGES = 16
KV_SLOTS = 3
FAR_GROUP = 4


def _sigmoid(x):
    return 1.0 / (1.0 + jnp.exp(-x))


def _gelu_tanh(x):
    c = math.sqrt(2.0 / math.pi)
    return x * (0.5 * (1.0 + jnp.tanh(c * (x + 0.044715 * (x * x * x)))))


def _layer_norm(x, g, b):
    mu = jnp.mean(x, axis=-1, keepdims=True)
    xc = x - mu
    var = jnp.mean(xc * xc, axis=-1, keepdims=True)
    return xc * lax.rsqrt(var + LN_EPS) * g + b


def _lru_coeffs(xr, wa_ref, ba_ref, wx_ref, bx_ref, lam_ref):
    xb = xr.astype(BF16)
    r = _sigmoid(jnp.dot(xb, wa_ref[...], preferred_element_type=F32) + ba_ref[...])
    i = _sigmoid(jnp.dot(xb, wx_ref[...], preferred_element_type=F32) + bx_ref[...])
    nl = -lam_ref[...]
    softplus = jnp.maximum(nl, 0.0) + jnp.log1p(jnp.exp(-jnp.abs(nl)))
    log_a = (-LRU_C) * r * softplus
    a = jnp.exp(log_a)
    u = jnp.sqrt(-jnp.tanh(log_a) * (a * a + 1.0)) * (i * xr)
    return a, u


def _rel_bucket(dist):
    n = jnp.maximum(dist, 0)
    max_exact = REL_BUCKETS // 2
    nf = jnp.maximum(n, 1).astype(F32)
    large = max_exact + (jnp.log(nf / max_exact) / math.log(REL_MAX_DIST / max_exact)
                         * (REL_BUCKETS - max_exact)).astype(I32)
    return jnp.where(n < max_exact, n, jnp.minimum(large, REL_BUCKETS - 1))


def _bias_lookup(rb_ref, bucket, h):
    val = jnp.zeros(bucket.shape, F32)
    for k in range(REL_BUCKETS):
        val = jnp.where(bucket == k, rb_ref[k, h], val)
    return val


def _bias_tables_kernel(rb_ref, t0_ref, t1_ref, bs_ref):
    h = pl.program_id(0)
    kk = lax.broadcasted_iota(I32, (MOBA_BLOCK, MOBA_BLOCK), 0)
    qq = lax.broadcasted_iota(I32, (MOBA_BLOCK, MOBA_BLOCK), 1)
    d0 = qq - kk
    t0_ref[0] = jnp.where(d0 >= 0, _bias_lookup(rb_ref, _rel_bucket(d0), h) * LOG2E, NEG_INF)
    t1_ref[0] = _bias_lookup(rb_ref, _rel_bucket(d0 + MOBA_BLOCK), h) * LOG2E
    ds = MOBA_BLOCK - lax.broadcasted_iota(I32, (SUBLANES, MOBA_BLOCK), 1)
    bs_ref[0] = _bias_lookup(rb_ref, _rel_bucket(ds), h)


def _bias_tables(rel_bias):
    tbl = jax.ShapeDtypeStruct((ATTN_HEADS, MOBA_BLOCK, MOBA_BLOCK), F32)
    return pl.pallas_call(
        _bias_tables_kernel,
        grid=(ATTN_HEADS,),
        in_specs=[pl.BlockSpec(memory_space=pltpu.SMEM)],
        out_specs=[pl.BlockSpec((1, MOBA_BLOCK, MOBA_BLOCK), lambda h: (h, 0, 0)),
                   pl.BlockSpec((1, MOBA_BLOCK, MOBA_BLOCK), lambda h: (h, 0, 0)),
                   pl.BlockSpec((1, SUBLANES, MOBA_BLOCK), lambda h: (h, 0, 0))],
        out_shape=[tbl, tbl, jax.ShapeDtypeStruct((ATTN_HEADS, SUBLANES, MOBA_BLOCK), F32)],
        name="bias_tables",
    )(rel_bias)


def _inproj_prompt_kernel(x_ref, w_ref, zc_ref, zl_ref, kt_ref, vt_ref, qtb_ref, kb_ref, vtb_ref, km_ref):
    xb = x_ref[...].astype(BF16)
    tm = xb.shape[0]

    def col(c):
        return jnp.dot(xb, w_ref[:, c * ATTN_WIDTH:(c + 1) * ATTN_WIDTH], preferred_element_type=F32)

    zc_ref[...] = col(0)
    zl_ref[...] = col(1)
    qtb_ref[0] = (col(2) * (Q_SCALE * LOG2E)).T.astype(BF16)
    k = col(3)
    kt_ref[0] = k.T
    kb_ref[...] = k.astype(BF16)
    for r in range(tm // MOBA_BLOCK):
        km_ref[r] = jnp.sum(k[r * MOBA_BLOCK:(r + 1) * MOBA_BLOCK], axis=0, keepdims=True) * (1.0 / MOBA_BLOCK)
    vt = col(4).T
    vt_ref[0] = vt
    ones = jnp.ones((VT_ROWS - HEAD_DIM, tm), BF16)
    for h in range(ATTN_HEADS):
        vtb_ref[0, h, 0:HEAD_DIM, :] = vt[h * HEAD_DIM:(h + 1) * HEAD_DIM].astype(BF16)
        vtb_ref[0, h, HEAD_DIM:VT_ROWS, :] = ones


def _inproj_prompt(x, w, nbatch, seq):
    m = x.shape[0]
    tm = ROW_TILE
    nb = seq // tm
    nblk = seq // MOBA_BLOCK
    rows = lambda i: (i, 0)
    tcols = lambda i: (i // nb, 0, i % nb)
    return pl.pallas_call(
        _inproj_prompt_kernel,
        grid=(m // tm,),
        in_specs=[pl.BlockSpec((tm, D_MODEL), rows),
                  pl.BlockSpec(w.shape, lambda i: (0, 0))],
        out_specs=[pl.BlockSpec((tm, ATTN_WIDTH), rows),
                   pl.BlockSpec((tm, ATTN_WIDTH), rows),
                   pl.BlockSpec((1, ATTN_WIDTH, tm), tcols),
                   pl.BlockSpec((1, ATTN_WIDTH, tm), tcols),
                   pl.BlockSpec((1, ATTN_WIDTH, tm), tcols),
                   pl.BlockSpec((tm, ATTN_WIDTH), rows),
                   pl.BlockSpec((1, ATTN_HEADS, VT_ROWS, tm), lambda i: (i // nb, 0, 0, i % nb)),
                   pl.BlockSpec((tm // MOBA_BLOCK, 1, ATTN_WIDTH), lambda i: (i, 0, 0))],
        out_shape=[jax.ShapeDtypeStruct((m, ATTN_WIDTH), F32),
                   jax.ShapeDtypeStruct((m, ATTN_WIDTH), F32),
                   jax.ShapeDtypeStruct((nbatch, ATTN_WIDTH, seq), F32),
                   jax.ShapeDtypeStruct((nbatch, ATTN_WIDTH, seq), F32),
                   jax.ShapeDtypeStruct((nbatch, ATTN_WIDTH, seq), BF16),
                   jax.ShapeDtypeStruct((m, ATTN_WIDTH), BF16),
                   jax.ShapeDtypeStruct((nbatch, ATTN_HEADS, VT_ROWS, seq), BF16),
                   jax.ShapeDtypeStruct((m // MOBA_BLOCK, 1, ATTN_WIDTH), F32)],
        compiler_params=pltpu.CompilerParams(dimension_semantics=("parallel",)),
        name="inproj_prompt",
    )(x, w)


def _inproj_sample_kernel(x_ref, w_ref, zc_ref, zl_ref, q_ref, k_ref, v_ref):
    xb = x_ref[...].astype(BF16)
    for c, o_ref in enumerate((zc_ref, zl_ref, q_ref, k_ref, v_ref)):
        o_ref[...] = jnp.dot(xb, w_ref[:, c * ATTN_WIDTH:(c + 1) * ATTN_WIDTH], preferred_element_type=F32)


def _inproj_sample(x, w):
    m = x.shape[0]
    out = jax.ShapeDtypeStruct((m, ATTN_WIDTH), F32)
    return pl.pallas_call(_inproj_sample_kernel, out_shape=[out] * 5, name="inproj_sample")(x, w)


def _mixer_prompt_kernel(zc_ref, zl_ref, cw_ref, cb_ref, lg_ref, lb_ref, lw_ref, lcb_ref,
                         wa_ref, ba_ref, wx_ref, bx_ref, lam_ref,
                         oab_ref, cst_ref, lst_ref, hst_ref,
                         glu_ext, glu_sh, xl_ext, h_sc):
    i = pl.program_id(1)
    tc = zc_ref.shape[0]

    @pl.when(i == 0)
    def _():
        glu_ext[0:CONV_HALO, :] = jnp.zeros((CONV_HALO, CONV_CH), F32)
        xl_ext[0:LRU_HALO, :] = jnp.zeros((LRU_HALO, LRU_CH), F32)
        h_sc[...] = jnp.zeros(h_sc.shape, F32)

    glu_ext[CONV_HALO:CONV_HALO + tc, :] = zc_ref[:, 0:CONV_CH] * _sigmoid(zc_ref[:, CONV_CH:2 * CONV_CH])
    first = CONV_HALO - (CONV_WIDTH - 1)
    span = tc + CONV_HALO - SUBLANES
    for r in range(1, SUBLANES):
        for k0 in range(0, span, CONV_CHUNK):
            n = min(CONV_CHUNK, span - k0)
            glu_sh[r - 1, k0:k0 + n, :] = glu_ext[pl.ds(k0 + r, n), :]
    for c in range(tc // CONV_CHUNK):
        acc = jnp.broadcast_to(cb_ref[...], (CONV_CHUNK, CONV_CH))
        for j in range(CONV_WIDTH):
            r = (first + j) % SUBLANES
            row0 = c * CONV_CHUNK + first + j - r
            tap = glu_ext[row0:row0 + CONV_CHUNK, :] if r == 0 else glu_sh[r - 1, row0:row0 + CONV_CHUNK, :]
            acc = acc + cw_ref[j:j + 1, :] * tap
        ln = _layer_norm(acc, lg_ref[...], lb_ref[...])
        oab_ref[c * CONV_CHUNK:(c + 1) * CONV_CHUNK, 0:CONV_CH] = ln * _sigmoid(ln)

    xl_ext[LRU_HALO:LRU_HALO + tc, :] = zl_ref[:, LRU_CH:2 * LRU_CH]
    lfirst = LRU_HALO - (LRU_CONV_WIDTH - 1)
    xr = jnp.broadcast_to(lcb_ref[...], (tc, LRU_CH))
    for j in range(LRU_CONV_WIDTH):
        xr = xr + lw_ref[j:j + 1, :] * xl_ext[pl.ds(lfirst + j, tc), :]
    a, u = _lru_coeffs(xr, wa_ref, ba_ref, wx_ref, bx_ref, lam_ref)
    row = lax.broadcasted_iota(I32, (tc, LRU_CH), 0)
    s = 1
    while s < tc:
        keep = row >= s
        u = jnp.where(keep, a * pltpu.roll(u, s, 0) + u, u)
        a = jnp.where(keep, a * pltpu.roll(a, s, 0), a)
        s *= 2
    h = a * h_sc[0:1, :] + u
    oab_ref[:, CONV_CH:CONV_CH + LRU_CH] = _gelu_tanh(zl_ref[:, 0:LRU_CH]) * h
    h_last = h[tc - 1:tc, :]

    @pl.when(i == pl.num_programs(1) - 1)
    def _():
        cst_ref[0] = glu_ext[pl.ds(tc + first, CONV_WIDTH - 1), :]
        lst_ref[0] = xl_ext[pl.ds(tc + lfirst, LRU_CONV_WIDTH - 1), :]
        hst_ref[0] = h_last

    glu_ext[0:CONV_HALO, :] = glu_ext[tc:tc + CONV_HALO, :]
    xl_ext[0:LRU_HALO, :] = xl_ext[tc:tc + LRU_HALO, :]
    h_sc[...] = jnp.broadcast_to(h_last, h_sc.shape)


def _mixer_prompt(zc, zl, p, nbatch, seq):
    tc = ROW_TILE
    nt = seq // tc
    rows = lambda b, i: (b * nt + i, 0)
    full = lambda a: pl.BlockSpec(a.shape, lambda b, i: (0,) * a.ndim)
    params = [p["conv_dw_w"], p["conv_dw_b"], p["conv_ln_g"], p["conv_ln_b"], p["lru_conv_w"], p["lru_conv_b"],
              p["wa"], p["lru_ba"], p["wx"], p["lru_bx"], p["lru_lambda"]]
    return pl.pallas_call(
        _mixer_prompt_kernel,
        grid=(nbatch, nt),
        in_specs=[pl.BlockSpec((tc, 2 * CONV_CH), rows), pl.BlockSpec((tc, 2 * LRU_CH), rows)]
        + [full(a) for a in params],
        out_specs=[pl.BlockSpec((tc, CONV_CH + LRU_CH), rows),
                   pl.BlockSpec((1, CONV_WIDTH - 1, CONV_CH), lambda b, i: (b, 0, 0)),
                   pl.BlockSpec((1, LRU_CONV_WIDTH - 1, LRU_CH), lambda b, i: (b, 0, 0)),
                   pl.BlockSpec((1, 1, LRU_CH), lambda b, i: (b, 0, 0))],
        out_shape=[jax.ShapeDtypeStruct((nbatch * seq, CONV_CH + LRU_CH), F32),
                   jax.ShapeDtypeStruct((nbatch, CONV_WIDTH - 1, CONV_CH), F32),
                   jax.ShapeDtypeStruct((nbatch, LRU_CONV_WIDTH - 1, LRU_CH), F32),
                   jax.ShapeDtypeStruct((nbatch, 1, LRU_CH), F32)],
        scratch_shapes=[pltpu.VMEM((tc + CONV_HALO, CONV_CH), F32),
                        pltpu.VMEM((SUBLANES - 1, tc + CONV_HALO - SUBLANES, CONV_CH), F32),
                        pltpu.VMEM((tc + LRU_HALO, LRU_CH), F32),
                        pltpu.VMEM((SUBLANES, LRU_CH), F32)],
        compiler_params=pltpu.CompilerParams(dimension_semantics=("parallel", "arbitrary")),
        name="mixer_prompt",
    )(zc, zl, *params)


def _mixer_sample_kernel(zc_ref, zl_ref, cs_ref, ls_ref, h0_ref, cw_ref, cb_ref, lg_ref, lb_ref, lw_ref, lcb_ref,
                         wa_ref, ba_ref, wx_ref, bx_ref, lam_ref,
                         oab_ref, cst_ref, lst_ref, hst_ref):
    glu = zc_ref[:, 0:CONV_CH] * _sigmoid(zc_ref[:, CONV_CH:2 * CONV_CH])
    acc = cb_ref[...] + cw_ref[CONV_WIDTH - 1:CONV_WIDTH, :] * glu
    for j in range(CONV_WIDTH - 1):
        acc = acc + cw_ref[j:j + 1, :] * cs_ref[j]
        if j >= 1:
            cst_ref[j - 1] = cs_ref[j]
    cst_ref[CONV_WIDTH - 2] = glu
    ln = _layer_norm(acc, lg_ref[...], lb_ref[...])
    oab_ref[:, 0:CONV_CH] = ln * _sigmoid(ln)

    xl = zl_ref[:, LRU_CH:2 * LRU_CH]
    xr = lcb_ref[...] + lw_ref[LRU_CONV_WIDTH - 1:LRU_CONV_WIDTH, :] * xl
    for j in range(LRU_CONV_WIDTH - 1):
        xr = xr + lw_ref[j:j + 1, :] * ls_ref[j]
        if j >= 1:
            lst_ref[j - 1] = ls_ref[j]
    lst_ref[LRU_CONV_WIDTH - 2] = xl
    a, u = _lru_coeffs(xr, wa_ref, ba_ref, wx_ref, bx_ref, lam_ref)
    h = a * h0_ref[...] + u
    hst_ref[...] = h
    oab_ref[:, CONV_CH:CONV_CH + LRU_CH] = _gelu_tanh(zl_ref[:, 0:LRU_CH]) * h


def _mixer_sample(zc, zl, conv_state, lru_state, h0, p):
    ns = zc.shape[0]
    params = [p["conv_dw_w"], p["conv_dw_b"], p["conv_ln_g"], p["conv_ln_b"], p["lru_conv_w"], p["lru_conv_b"],
              p["wa"], p["lru_ba"], p["wx"], p["lru_bx"], p["lru_lambda"]]
    return pl.pallas_call(
        _mixer_sample_kernel,
        out_shape=[jax.ShapeDtypeStruct((ns, CONV_CH + LRU_CH), F32),
                   jax.ShapeDtypeStruct(conv_state.shape, F32),
                   jax.ShapeDtypeStruct(lru_state.shape, F32),
                   jax.ShapeDtypeStruct((ns, LRU_CH), F32)],
        name="mixer_sample",
    )(zc, zl, conv_state, lru_state, h0, *params)


def _select_prompt_kernel(rb_ref, km_ref, qt_ref, madd_ref):
    i = pl.program_id(1)
    nbp = madd_ref.shape[2] - SUBLANES
    gate = jnp.dot(km_ref[0], qt_ref[0].astype(F32), preferred_element_type=F32,
                   precision=lax.Precision.HIGHEST)
    n = lax.broadcasted_iota(I32, (nbp, MOBA_BLOCK), 0)
    nf = n.astype(F32)
    past = n < i
    tail_row = lax.broadcasted_iota(I32, (SUBLANES, MOBA_BLOCK), 0)
    for h in range(ATTN_HEADS):
        g = jnp.where(past, gate[h * nbp:(h + 1) * nbp], NEG_INF)
        sel = jnp.zeros((nbp, MOBA_BLOCK), jnp.bool_)
        for _ in range(MOBA_TOPK):
            best = jnp.max(g, axis=0, keepdims=True)
            idx = jnp.min(jnp.where(g == best, nf, float(nbp)), axis=0, keepdims=True)
            pick = nf == idx
            sel = jnp.logical_or(sel, pick)
            g = jnp.where(pick, -jnp.inf, g)
        far_bias = rb_ref[REL_BUCKETS - 1, h] * LOG2E
        madd_ref[0, h, 0:nbp, :] = jnp.where(jnp.logical_and(sel, n < i - 1), far_bias, NEG_INF)
        prev = jnp.max(jnp.where(jnp.logical_and(sel, n == i - 1), 0.0, NEG_INF), axis=0, keepdims=True)
        madd_ref[0, h, nbp:nbp + SUBLANES, :] = jnp.where(tail_row == 0, prev, NEG_INF)


def _select_prompt(rel_bias, kmbd, qtb, nbp):
    nbatch, _, seq = qtb.shape
    nblk = seq // MOBA_BLOCK
    rows = nbp + SUBLANES
    return pl.pallas_call(
        _select_prompt_kernel,
        grid=(nbatch, nblk),
        in_specs=[pl.BlockSpec(memory_space=pltpu.SMEM),
                  pl.BlockSpec((1, ATTN_HEADS * nbp, ATTN_WIDTH), lambda b, i: (b, 0, 0)),
                  pl.BlockSpec((1, ATTN_WIDTH, MOBA_BLOCK), lambda b, i: (b, 0, i))],
        out_specs=pl.BlockSpec((1, ATTN_HEADS, rows, MOBA_BLOCK), lambda b, i: (b, 0, 0, i)),
        out_shape=jax.ShapeDtypeStruct((nbatch, ATTN_HEADS, rows, seq), F32),
        compiler_params=pltpu.CompilerParams(dimension_semantics=("parallel", "arbitrary")),
        name="select_prompt",
    )(rel_bias, kmbd, qtb)


def _far_schedule(nblk):
    qblk, base, first = [], [], []
    for i in range(nblk):
        for g in range((i + FAR_GROUP - 2) // FAR_GROUP):
            qblk.append(i)
            base.append(g * FAR_GROUP)
            first.append(1 if g == 0 else 0)
    return np.array([qblk, base, first], np.int32)


def _attn_prompt_kernel(sched_ref, qt_ref, k_ref, vt_ref, madd_ref, t0_ref, t1_ref, o_ref,
                        sa_sc, sb_sc, pa_sc, pb_sc, mn_sc, acc_sc):
    nblk = qt_ref.shape[2] // MOBA_BLOCK
    nbp = madd_ref.shape[2] - SUBLANES
    ntile = sched_ref.shape[1]
    width = 2 * MOBA_BLOCK
    near = 2 * MOBA_BLOCK
    group = FAR_GROUP * MOBA_BLOCK

    def qcols(i):
        return pl.ds(pl.multiple_of(i * MOBA_BLOCK, MOBA_BLOCK), MOBA_BLOCK)

    def both_heads(i):
        qt = qt_ref[0, :, qcols(i)]
        z = jnp.zeros((HEAD_DIM, MOBA_BLOCK), BF16)
        return jnp.concatenate([jnp.concatenate([qt[0:HEAD_DIM], z], axis=1),
                                jnp.concatenate([z, qt[HEAD_DIM:2 * HEAD_DIM]], axis=1)], axis=0)

    def lanes2(fn):
        return jnp.concatenate([fn(0), fn(1)], axis=1)

    def pv(p_ref, rows0, nrows):
        return jnp.stack([jnp.dot(vt_ref[0, e, :, pl.ds(rows0, nrows)],
                                  p_ref[0:nrows, e * MOBA_BLOCK:(e + 1) * MOBA_BLOCK],
                                  preferred_element_type=F32) for e in range(2)])

    buf_a, buf_b = (sa_sc, pa_sc), (sb_sc, pb_sc)

    def near_rows(i):
        return pl.multiple_of(jnp.maximum(i - 1, 0) * MOBA_BLOCK, MOBA_BLOCK)

    def near_x(i, buf, first=False):
        s_ref, _ = buf
        r0 = 0 if first else near_rows(i)
        qb = both_heads(i)
        cmax = None
        for c in range(2):
            s = jnp.dot(k_ref[pl.ds(r0 + c * MOBA_BLOCK, MOBA_BLOCK), :], qb, preferred_element_type=F32)
            if first:
                s = s + (lanes2(lambda e: t0_ref[e]) if c == 0 else NEG_INF)
            elif c == 0:
                s = s + (lanes2(lambda e: t1_ref[e]) + lanes2(lambda e: madd_ref[0, e, nbp:nbp + 1, qcols(i)]))
            else:
                s = s + lanes2(lambda e: t0_ref[e])
            s_ref[c * MOBA_BLOCK:(c + 1) * MOBA_BLOCK, :] = s
            bmax = jnp.max(s, axis=0, keepdims=True)
            cmax = bmax if cmax is None else jnp.maximum(cmax, bmax)
        return cmax

    def near_y(i, buf, m):
        s_ref, p_ref = buf
        for c in range(2):
            rows = slice(c * MOBA_BLOCK, (c + 1) * MOBA_BLOCK)
            p_ref[rows, :] = jnp.exp2(s_ref[rows, :] - m).astype(BF16)
        mn_sc[pl.ds(i, 1), :] = m

    def near_pv(i, buf):
        acc_sc[i] = pv(buf[1], near_rows(i), near)

    m_even = near_x(0, buf_a, first=True)
    m_odd = near_x(1, buf_b)
    near_y(0, buf_a, m_even)

    def near_body(h, m_odd):
        i = 2 * h + 1
        near_pv(i - 1, buf_a)
        m_even = near_x(i + 1, buf_a)
        near_y(i, buf_b, m_odd)
        near_pv(i, buf_b)
        m_odd = near_x(i + 2, buf_b)
        near_y(i + 1, buf_a, m_even)
        return m_odd

    m_odd = lax.fori_loop(0, nblk // 2 - 1, near_body, m_odd)
    near_pv(nblk - 2, buf_a)
    near_y(nblk - 1, buf_b, m_odd)
    near_pv(nblk - 1, buf_b)

    def tile(n):
        return sched_ref[0, n], sched_ref[1, n], sched_ref[2, n]

    def far_rows(i, base):
        return [lanes2(lambda e: madd_ref[0, e, pl.ds(base + c, 1), qcols(i)]) for c in range(FAR_GROUP)]

    def far_x(n, buf):
        s_ref, _ = buf
        i, base, _ = tile(n)
        qb = both_heads(i)

        def block(c):
            kc = k_ref[pl.ds(pl.multiple_of((base + c) * MOBA_BLOCK, MOBA_BLOCK), MOBA_BLOCK), :]
            s = jnp.dot(kc, qb, preferred_element_type=F32)
            s_ref[c * MOBA_BLOCK:(c + 1) * MOBA_BLOCK, :] = s
            return jnp.max(s, axis=0, keepdims=True)
        return block

    def far_state(n, bmax, m_prev):
        i, base, first = tile(n)
        m_in = jnp.where(first == 1, mn_sc[pl.ds(i, 1), :], m_prev)
        add = far_rows(i, base)
        m_new = m_in
        for c in range(FAR_GROUP):
            m_new = jnp.maximum(m_new, bmax[c] + add[c])
        return m_in, m_new

    def far_y(n, buf, m_new, x_next=None):
        s_ref, p_ref = buf
        i, base, _ = tile(n)
        add = far_rows(i, base)
        nxt = []
        for c in range(FAR_GROUP):
            if x_next is not None:
                nxt.append(x_next(c))
            rows = slice(c * MOBA_BLOCK, (c + 1) * MOBA_BLOCK)
            shift = m_new - add[c]
            if x_next is not None and c >= 1:
                shift = shift + 0.0 * nxt[c - 1]
            p_ref[rows, :] = jnp.exp2(s_ref[rows, :] - shift).astype(BF16)
        return nxt

    def far_pv(n, buf, alpha):
        i, base, _ = tile(n)
        scale = jnp.stack([jnp.broadcast_to(alpha[:, e * MOBA_BLOCK:(e + 1) * MOBA_BLOCK], (VT_ROWS, MOBA_BLOCK))
                           for e in range(2)])
        acc_sc[i] = scale * acc_sc[i] + pv(buf[1], pl.multiple_of(base * MOBA_BLOCK, group), group)

    if ntile:
        assert ntile % 2 == 0
        first_block = far_x(0, buf_a)
        m_in, m_new = far_state(0, [first_block(c) for c in range(FAR_GROUP)], jnp.zeros((1, width), F32))

        def step(n, cur, nxt, carry, has_prev=True, has_next=True):
            alpha_prev, m_in, m_new = carry
            bmax = far_y(n, cur, m_new, far_x(n + 1, nxt) if has_next else None)
            if has_prev:
                far_pv(n - 1, nxt, alpha_prev)
            alpha = jnp.exp2(m_in - m_new)
            if not has_next:
                return alpha
            return (alpha,) + far_state(n + 1, bmax, m_new)

        carry = step(0, buf_a, buf_b, (None, m_in, m_new), has_prev=False)

        def far_body(h, carry):
            carry = step(2 * h + 1, buf_b, buf_a, carry)
            return step(2 * h + 2, buf_a, buf_b, carry)

        carry = lax.fori_loop(0, ntile // 2 - 1, far_body, carry)
        alpha = step(ntile - 1, buf_b, buf_a, carry, has_next=False)
        far_pv(ntile - 1, buf_b, alpha)

    def finish(i, _):
        acc = acc_sc[i]
        ot = jnp.concatenate([acc[e, 0:HEAD_DIM] / acc[e, HEAD_DIM:HEAD_DIM + 1] for e in range(2)], axis=0)
        o_ref[pl.ds(pl.multiple_of(i * MOBA_BLOCK, MOBA_BLOCK), MOBA_BLOCK), :] = ot.T
        return 0

    lax.fori_loop(0, nblk, finish, 0)


def _attn_prompt(qtb, kb, vtb, madd, t0, t1):
    nbatch, _, seq = qtb.shape
    nblk = seq // MOBA_BLOCK
    rows = madd.shape[2]
    width = 2 * MOBA_BLOCK
    sched = _far_schedule(nblk)
    return pl.pallas_call(
        _attn_prompt_kernel,
        grid_spec=pltpu.PrefetchScalarGridSpec(
            num_scalar_prefetch=1,
            grid=(nbatch, HEAD_PAIRS),
            in_specs=[pl.BlockSpec((1, 2 * HEAD_DIM, seq), lambda b, p, s: (b, p, 0)),
                      pl.BlockSpec((seq, 2 * HEAD_DIM), lambda b, p, s: (b, p)),
                      pl.BlockSpec((1, 2, VT_ROWS, seq), lambda b, p, s: (b, p, 0, 0)),
                      pl.BlockSpec((1, 2, rows, seq), lambda b, p, s: (b, p, 0, 0)),
                      pl.BlockSpec((2, MOBA_BLOCK, MOBA_BLOCK), lambda b, p, s: (p, 0, 0)),
                      pl.BlockSpec((2, MOBA_BLOCK, MOBA_BLOCK), lambda b, p, s: (p, 0, 0))],
            out_specs=pl.BlockSpec((seq, 2 * HEAD_DIM), lambda b, p, s: (b, p)),
            scratch_shapes=[pltpu.VMEM((FAR_GROUP * MOBA_BLOCK, width), F32),
                            pltpu.VMEM((FAR_GROUP * MOBA_BLOCK, width), F32),
                            pltpu.VMEM((FAR_GROUP * MOBA_BLOCK, width), BF16),
                            pltpu.VMEM((FAR_GROUP * MOBA_BLOCK, width), BF16),
                            pltpu.VMEM((nblk, width), F32),
                            pltpu.VMEM((nblk, 2, VT_ROWS, MOBA_BLOCK), F32)]),
        out_shape=jax.ShapeDtypeStruct((nbatch * seq, ATTN_WIDTH), F32),
        compiler_params=pltpu.CompilerParams(dimension_semantics=("parallel", "parallel")),
        name="attn_prompt",
    )(jnp.asarray(sched), qtb, kb, vtb, madd, t0, t1)


def _make_kv_select_kernel(layer, n_pages):
    nchunk = n_pages // KV_CHUNK_PAGES
    blocks_per_chunk = KV_CHUNK_PAGES // PAGES_PER_BLOCK

    def page_copy(pt_ref, cache_ref, buf, sem, b, c, slot, p):
        page = pt_ref[b, c * KV_CHUNK_PAGES + p]
        return pltpu.make_async_copy(cache_ref.at[layer, page], buf.at[slot, p], sem.at[slot])

    def kernel(pt_ref, cache_ref, q_ref, sel_ref, buf, sem, kmt):
        b = pl.program_id(0)
        c = pl.program_id(1)
        step = b * nchunk + c
        nsteps = pl.num_programs(0) * nchunk
        slot = step % KV_SLOTS
        ahead = KV_SLOTS - 1

        def start(s):
            for p in range(KV_CHUNK_PAGES):
                page_copy(pt_ref, cache_ref, buf, sem, s // nchunk, s % nchunk, s % KV_SLOTS, p).start()

        @pl.when(step == 0)
        def _():
            for s in range(ahead):
                start(s)

        @pl.when(step + ahead < nsteps)
        def _():
            start(step + ahead)

        for p in range(KV_CHUNK_PAGES):
            page_copy(pt_ref, cache_ref, buf, sem, b, c, slot, p).wait()

        @pl.when(c == 0)
        def _():
            kmt[...] = jnp.zeros(kmt.shape, F32)

        lane = lax.broadcasted_iota(I32, kmt.shape, 1)
        km = kmt[...]
        for kb in range(blocks_per_chunk):
            tot = buf[slot, PAGES_PER_BLOCK * kb]
            for r in range(1, PAGES_PER_BLOCK):
                tot = tot + buf[slot, PAGES_PER_BLOCK * kb + r]
            mean = jnp.sum(tot.reshape(ATTN_WIDTH, PAGE_SIZE), axis=1, keepdims=True) * (1.0 / MOBA_BLOCK)
            km = jnp.where(lane == c * blocks_per_chunk + kb, mean, km)
        kmt[...] = km

        @pl.when(c == nchunk - 1)
        def _():
            nblk = n_pages // PAGES_PER_BLOCK
            hrow = lax.broadcasted_iota(I32, (SUBLANES, ATTN_WIDTH), 0)
            hcol = lax.broadcasted_iota(I32, (SUBLANES, ATTN_WIDTH), 1) // HEAD_DIM
            qbd = jnp.where(hrow == hcol, jnp.broadcast_to(q_ref[0], (SUBLANES, ATTN_WIDTH)), 0.0)
            gate = jnp.dot(qbd, km, preferred_element_type=F32, precision=lax.Precision.HIGHEST)
            nf = lax.broadcasted_iota(I32, gate.shape, 1).astype(F32)
            g = jnp.where(nf < nblk, gate, -jnp.inf)
            for r in range(MOBA_TOPK):
                best = jnp.max(g, axis=1, keepdims=True)
                idx = jnp.min(jnp.where(g == best, nf, float(LANES)), axis=1, keepdims=True)
                sel_ref[0, r] = jnp.broadcast_to(idx, (SUBLANES, LANES)).astype(I32)
                g = jnp.where(nf == idx, -jnp.inf, g)

    return kernel


def _kv_select(page_table, cache_t, q3, layer):
    ns, n_pages = page_table.shape
    nchunk = n_pages // KV_CHUNK_PAGES
    return pl.pallas_call(
        _make_kv_select_kernel(layer, n_pages),
        grid_spec=pltpu.PrefetchScalarGridSpec(
            num_scalar_prefetch=1,
            grid=(ns, nchunk),
            in_specs=[pl.BlockSpec(memory_space=pl.ANY),
                      pl.BlockSpec((1, 1, ATTN_WIDTH), lambda b, c, pt: (b, 0, 0))],
            out_specs=pl.BlockSpec((1, MOBA_TOPK, SUBLANES, LANES), lambda b, c, pt: (b, 0, 0, 0)),
            scratch_shapes=[pltpu.VMEM((KV_SLOTS, KV_CHUNK_PAGES, ATTN_HEADS, HEAD_DIM, PAGE_SIZE), F32),
                            pltpu.SemaphoreType.DMA((KV_SLOTS,)),
                            pltpu.VMEM((ATTN_WIDTH, LANES), F32)]),
        out_shape=jax.ShapeDtypeStruct((ns, MOBA_TOPK, SUBLANES, LANES), I32),
        compiler_params=pltpu.CompilerParams(dimension_semantics=("arbitrary", "arbitrary")),
        name="kv_select",
    )(page_table, cache_t, q3)


def _make_attn_sample_kernel(layer, n_pages):
    nblk = n_pages // PAGES_PER_BLOCK
    nsel = MOBA_TOPK * MOBA_BLOCK

    def copies(pt_ref, sel_ref, ck_ref, cv_ref, kbuf, vbuf, sem, b, slot):
        out = []
        for h in range(ATTN_HEADS):
            for r in range(MOBA_TOPK):
                blk = sel_ref[b, h * MOBA_TOPK + r]
                for half in range(PAGES_PER_BLOCK):
                    page = pt_ref[b, blk * PAGES_PER_BLOCK + half]
                    dst = pl.ds(r * MOBA_BLOCK + half * PAGE_SIZE, PAGE_SIZE)
                    out.append(pltpu.make_async_copy(ck_ref.at[layer, page, h], kbuf.at[slot, h, :, dst], sem.at[0, slot]))
                    out.append(pltpu.make_async_copy(cv_ref.at[layer, page, h], vbuf.at[slot, h, :, dst], sem.at[1, slot]))
        return out

    def kernel(pt_ref, sel_ref, rb_ref, ck_ref, cv_ref, q_ref, kn_ref, vn_ref, bs_ref, o_ref, kbuf, vbuf, sem):
        b = pl.program_id(0)
        slot = b % 2

        @pl.when(b == 0)
        def _():
            for cp in copies(pt_ref, sel_ref, ck_ref, cv_ref, kbuf, vbuf, sem, 0, 0):
                cp.start()

        @pl.when(b + 1 < pl.num_programs(0))
        def _():
            for cp in copies(pt_ref, sel_ref, ck_ref, cv_ref, kbuf, vbuf, sem, b + 1, 1 - slot):
                cp.start()

        for cp in copies(pt_ref, sel_ref, ck_ref, cv_ref, kbuf, vbuf, sem, b, slot):
            cp.wait()

        q = q_ref[0] * Q_SCALE
        kn = kn_ref[0]
        vn = vn_ref[0]
        for h in range(ATTN_HEADS):
            hs = slice(h * HEAD_DIM, (h + 1) * HEAD_DIM)
            qh = q[:, hs]
            q8 = jnp.broadcast_to(qh, (SUBLANES, HEAD_DIM)).astype(BF16)
            s = jnp.dot(q8, kbuf[slot, h].astype(BF16), preferred_element_type=F32)
            far_bias = rb_ref[REL_BUCKETS - 1, h]
            bias = jnp.concatenate(
                [jnp.where(sel_ref[b, h * MOBA_TOPK + r] == nblk - 1, bs_ref[h], far_bias)
                 for r in range(MOBA_TOPK)], axis=1)
            s = s + bias
            s_new = jnp.sum(qh * kn[:, hs], axis=1, keepdims=True) + rb_ref[0, h]
            m = jnp.maximum(jnp.max(s, axis=1, keepdims=True), s_new)
            p = jnp.exp(s - m)
            p_new = jnp.exp(s_new - m)
            l = jnp.sum(p, axis=1, keepdims=True) + p_new
            o = lax.dot_general(p.astype(BF16), vbuf[slot, h].astype(BF16), (((1,), (1,)), ((), ())),
                                preferred_element_type=F32)
            o = (o + p_new * vn[:, hs]) / l
            o_ref[0, :, hs] = o[0:1]

    return kernel


def _attn_sample(page_table, sel, rel_bias, cache_kt, cache_vt, q3, k3, v3, bs, layer):
    ns, n_pages = page_table.shape
    nsel = MOBA_TOPK * MOBA_BLOCK
    row = pl.BlockSpec((1, 1, ATTN_WIDTH), lambda b, pt, sl: (b, 0, 0))
    return pl.pallas_call(
        _make_attn_sample_kernel(layer, n_pages),
        grid_spec=pltpu.PrefetchScalarGridSpec(
            num_scalar_prefetch=2,
            grid=(ns,),
            in_specs=[pl.BlockSpec(memory_space=pltpu.SMEM),
                      pl.BlockSpec(memory_space=pl.ANY),
                      pl.BlockSpec(memory_space=pl.ANY),
                      row, row, row,
                      pl.BlockSpec(bs.shape, lambda b, pt, sl: (0, 0, 0))],
            out_specs=row,
            scratch_shapes=[pltpu.VMEM((2, ATTN_HEADS, HEAD_DIM, nsel), F32),
                            pltpu.VMEM((2, ATTN_HEADS, HEAD_DIM, nsel), F32),
                            pltpu.SemaphoreType.DMA((2, 2))]),
        out_shape=jax.ShapeDtypeStruct((ns, 1, ATTN_WIDTH), F32),
        compiler_params=pltpu.CompilerParams(dimension_semantics=("arbitrary",)),
        name="attn_sample",
    )(page_table, sel, rel_bias, cache_kt, cache_vt, q3, k3, v3, bs)


def _outproj_kernel(ab_ref, c_ref, x_ref, w_ref, g_ref, b_ref, o_ref):
    split = CONV_CH + LRU_CH
    y = jnp.dot(ab_ref[...].astype(BF16), w_ref[0:split, :], preferred_element_type=F32)
    y = y + jnp.dot(c_ref[...].astype(BF16), w_ref[split:split + ATTN_WIDTH, :], preferred_element_type=F32)
    o_ref[...] = _layer_norm(DEEPNORM_ALPHA * x_ref[...] + y, g_ref[...], b_ref[...])


def _outproj(oab, oc, x, w, g, b):
    m = x.shape[0]
    tm = min(ROW_TILE, m)
    rows = lambda i: (i, 0)
    const = lambda a: pl.BlockSpec(a.shape, lambda i: (0, 0))
    return pl.pallas_call(
        _outproj_kernel,
        grid=(m // tm,),
        in_specs=[pl.BlockSpec((tm, CONV_CH + LRU_CH), rows), pl.BlockSpec((tm, ATTN_WIDTH), rows),
                  pl.BlockSpec((tm, D_MODEL), rows), const(w), const(g), const(b)],
        out_specs=pl.BlockSpec((tm, D_MODEL), rows),
        out_shape=jax.ShapeDtypeStruct((m, D_MODEL), F32),
        compiler_params=pltpu.CompilerParams(dimension_semantics=("parallel",)),
        name="outproj",
    )(oab, oc, x, w, g, b)


def _make_ffn_prompt_kernel(tiles_per_seq):
    halo = BF16_ROWS
    taps = FFN_CONV_WIDTH

    def kernel(x_ref, xh_ref, wu_ref, wg_ref, dw_ref, db_ref, wd_ref, g_ref, b_ref, o_ref, fst_ref, xb_sc, up_sc, acc_sc):
        i = pl.program_id(0)
        n = pl.program_id(1)
        tm = x_ref.shape[0]

        @pl.when(n == 0)
        def _():
            keep = jnp.where(i % tiles_per_seq == 0, 0.0, 1.0)
            xb_sc[0:halo, :] = (xh_ref[...] * keep).astype(BF16)
            xb_sc[halo:halo + tm, :] = x_ref[...].astype(BF16)
            acc_sc[...] = jnp.zeros(acc_sc.shape, F32)

        up_sc[...] = jnp.dot(xb_sc[...], wu_ref[...], preferred_element_type=F32)
        y = jnp.broadcast_to(db_ref[...], (tm, up_sc.shape[1]))
        for j in range(taps):
            y = y + dw_ref[j:j + 1, :] * up_sc[pl.ds(halo - (taps - 1) + j, tm), :]
        fst_ref[0] = up_sc[pl.ds(halo + tm - (taps - 1), taps - 1), :]
        gate = jnp.dot(xb_sc[halo:halo + tm, :], wg_ref[...], preferred_element_type=F32)
        acc_sc[...] += jnp.dot((_gelu_tanh(y) * gate).astype(BF16), wd_ref[...], preferred_element_type=F32)

        @pl.when(n == pl.num_programs(1) - 1)
        def _():
            o_ref[...] = _layer_norm(DEEPNORM_ALPHA * x_ref[...] + acc_sc[...], g_ref[...], b_ref[...])

    return kernel


def _ffn_prompt(x, p, nbatch, seq):
    m = x.shape[0]
    tm, tn = FFN_ROW_TILE, FF_TILE
    nt = seq // tm
    halo = BF16_ROWS
    hb = tm // halo
    y, tails = pl.pallas_call(
        _make_ffn_prompt_kernel(nt),
        grid=(m // tm, D_FF // tn),
        in_specs=[pl.BlockSpec((tm, D_MODEL), lambda i, n: (i, 0)),
                  pl.BlockSpec((halo, D_MODEL), lambda i, n: (jnp.maximum(i * hb - 1, 0), 0)),
                  pl.BlockSpec((D_MODEL, tn), lambda i, n: (0, n)),
                  pl.BlockSpec((D_MODEL, tn), lambda i, n: (0, n)),
                  pl.BlockSpec((FFN_CONV_WIDTH, tn), lambda i, n: (0, n)),
                  pl.BlockSpec((1, tn), lambda i, n: (0, n)),
                  pl.BlockSpec((tn, D_MODEL), lambda i, n: (n, 0)),
                  pl.BlockSpec((1, D_MODEL), lambda i, n: (0, 0)),
                  pl.BlockSpec((1, D_MODEL), lambda i, n: (0, 0))],
        out_specs=[pl.BlockSpec((tm, D_MODEL), lambda i, n: (i, 0)),
                   pl.BlockSpec((1, FFN_CONV_WIDTH - 1, tn), lambda i, n: (i, 0, n))],
        out_shape=[jax.ShapeDtypeStruct((m, D_MODEL), F32),
                   jax.ShapeDtypeStruct((m // tm, FFN_CONV_WIDTH - 1, D_FF), F32)],
        scratch_shapes=[pltpu.VMEM((tm + halo, D_MODEL), BF16),
                        pltpu.VMEM((tm + halo, tn), F32),
                        pltpu.VMEM((tm, D_MODEL), F32)],
        compiler_params=pltpu.CompilerParams(dimension_semantics=("arbitrary", "arbitrary")),
        name="ffn_prompt",
    )(x, x, p["w_up"], p["w_gate"], p["ffn_dw_w"], p["ffn_dw_b"], p["w_down"], p["ln2_g"], p["ln2_b"])
    return y, tails[nt - 1::nt]


def _ffn_sample_kernel(x_ref, st_ref, wu_ref, wg_ref, dw_ref, db_ref, wd_ref, g_ref, b_ref, o_ref, fst_ref, acc_sc):
    n = pl.program_id(0)
    taps = FFN_CONV_WIDTH

    @pl.when(n == 0)
    def _():
        acc_sc[...] = jnp.zeros(acc_sc.shape, F32)

    xb = x_ref[...].astype(BF16)
    up = jnp.dot(xb, wu_ref[...], preferred_element_type=F32)
    y = db_ref[...] + dw_ref[taps - 1:taps, :] * up
    for j in range(taps - 1):
        y = y + dw_ref[j:j + 1, :] * st_ref[j]
        if j >= 1:
            fst_ref[j - 1] = st_ref[j]
    fst_ref[taps - 2] = up
    gate = jnp.dot(xb, wg_ref[...], preferred_element_type=F32)
    acc_sc[...] += jnp.dot((_gelu_tanh(y) * gate).astype(BF16), wd_ref[...], preferred_element_type=F32)

    @pl.when(n == pl.num_programs(0) - 1)
    def _():
        o_ref[...] = _layer_norm(DEEPNORM_ALPHA * x_ref[...] + acc_sc[...], g_ref[...], b_ref[...])


def _ffn_sample(x, state, p):
    ns = x.shape[0]
    tn = FF_TILE
    return pl.pallas_call(
        _ffn_sample_kernel,
        grid=(D_FF // tn,),
        in_specs=[pl.BlockSpec((ns, D_MODEL), lambda n: (0, 0)),
                  pl.BlockSpec((FFN_CONV_WIDTH - 1, ns, tn), lambda n: (0, 0, n)),
                  pl.BlockSpec((D_MODEL, tn), lambda n: (0, n)),
                  pl.BlockSpec((D_MODEL, tn), lambda n: (0, n)),
                  pl.BlockSpec((FFN_CONV_WIDTH, tn), lambda n: (0, n)),
                  pl.BlockSpec((1, tn), lambda n: (0, n)),
                  pl.BlockSpec((tn, D_MODEL), lambda n: (n, 0)),
                  pl.BlockSpec((1, D_MODEL), lambda n: (0, 0)),
                  pl.BlockSpec((1, D_MODEL), lambda n: (0, 0))],
        out_specs=[pl.BlockSpec((ns, D_MODEL), lambda n: (0, 0)),
                   pl.BlockSpec((FFN_CONV_WIDTH - 1, ns, tn), lambda n: (0, 0, n))],
        out_shape=[jax.ShapeDtypeStruct((ns, D_MODEL), F32),
                   jax.ShapeDtypeStruct((FFN_CONV_WIDTH - 1, ns, D_FF), F32)],
        scratch_shapes=[pltpu.VMEM((ns, D_MODEL), F32)],
        compiler_params=pltpu.CompilerParams(dimension_semantics=("arbitrary",)),
        name="ffn_sample",
    )(x, state, p["w_up"], p["w_gate"], p["ffn_dw_w"], p["ffn_dw_b"], p["w_down"], p["ln2_g"], p["ln2_b"])


def _block_diag(w):
    nb, n, _ = w.shape
    eye = jnp.eye(nb, dtype=w.dtype)
    return (eye[:, None, :, None] * w[:, :, None, :]).reshape(nb * n, nb * n)


def _layer_params(l, w_in, conv_dw_w, conv_dw_b, conv_ln_g, conv_ln_b, lru_conv_w, lru_conv_b, lru_wa, lru_ba,
                  lru_wx, lru_bx, lru_lambda, w_out, ln1_g, ln1_b, ffn_w_up, ffn_w_gate, ffn_dw_w, ffn_dw_b,
                  ffn_w_down, ln2_g, ln2_b):
    row = lambda a: a[l][None, :]
    return {
        "w_in": w_in[l].astype(BF16), "conv_dw_w": conv_dw_w[l], "conv_dw_b": row(conv_dw_b),
        "conv_ln_g": row(conv_ln_g), "conv_ln_b": row(conv_ln_b), "lru_conv_w": lru_conv_w[l],
        "lru_conv_b": row(lru_conv_b), "wa": _block_diag(lru_wa[l]).astype(BF16), "lru_ba": row(lru_ba),
        "wx": _block_diag(lru_wx[l]).astype(BF16), "lru_bx": row(lru_bx), "lru_lambda": row(lru_lambda),
        "w_out": w_out[l].astype(BF16), "ln1_g": row(ln1_g), "ln1_b": row(ln1_b),
        "w_up": ffn_w_up[l].astype(BF16), "w_gate": ffn_w_gate[l].astype(BF16), "ffn_dw_w": ffn_dw_w[l],
        "ffn_dw_b": row(ffn_dw_b), "w_down": ffn_w_down[l].astype(BF16), "ln2_g": row(ln2_g), "ln2_b": row(ln2_b),
    }


def _block_diag_means(kmean, nbatch, nblk, nbp):
    km = kmean.reshape(nbatch, nblk, ATTN_HEADS, HEAD_DIM).transpose(0, 2, 1, 3)
    km = jnp.pad(km, ((0, 0), (0, 0), (0, nbp - nblk), (0, 0)))
    eye = jnp.eye(ATTN_HEADS, dtype=F32)
    bd = km[:, :, :, None, :] * eye[None, :, None, :, None]
    return bd.reshape(nbatch, ATTN_HEADS * nbp, ATTN_WIDTH)


def kernel(x_prompt, x_sample, cache_k, cache_v, page_table, state_conv, state_lru_conv, state_lru_h, state_ffn_conv, w_in, conv_dw_w, conv_dw_b, conv_ln_g, conv_ln_b, lru_conv_w, lru_conv_b, lru_wa, lru_ba, lru_wx, lru_bx, lru_lambda, rel_bias, w_out, ln1_g, ln1_b, ffn_w_up, ffn_w_gate, ffn_dw_w, ffn_dw_b, ffn_w_down, ln2_g, ln2_b):
    nbatch, seq, _ = x_prompt.shape
    ns = x_sample.shape[0]
    depth = w_in.shape[0]
    nblk = seq // MOBA_BLOCK
    nbp = -(-nblk // SUBLANES) * SUBLANES
    assert seq % (FAR_GROUP * MOBA_BLOCK) == 0 and seq % ROW_TILE == 0 and x_sample.shape[1] == 1
    assert page_table.shape[1] % KV_CHUNK_PAGES == 0 and cache_k.shape[2] == PAGE_SIZE

    t0, t1, bs = _bias_tables(rel_bias)
    cache_kt = cache_k.transpose(0, 1, 3, 4, 2)
    cache_vt = cache_v.transpose(0, 1, 3, 4, 2)

    xp = x_prompt.reshape(nbatch * seq, D_MODEL)
    xs = x_sample.reshape(ns, D_MODEL)
    outs = {k: [] for k in ("k_p", "v_p", "k_s", "v_s", "conv_p", "conv_s", "lruc_p", "lruc_s",
                            "lruh_p", "lruh_s", "ffn_p", "ffn_s")}
    for l in range(depth):
        p = _layer_params(l, w_in, conv_dw_w, conv_dw_b, conv_ln_g, conv_ln_b, lru_conv_w, lru_conv_b, lru_wa,
                          lru_ba, lru_wx, lru_bx, lru_lambda, w_out, ln1_g, ln1_b, ffn_w_up, ffn_w_gate,
                          ffn_dw_w, ffn_dw_b, ffn_w_down, ln2_g, ln2_b)

        zc, zl, kt, vt, qtb, kb, vtb, kmean = _inproj_prompt(xp, p["w_in"], nbatch, seq)
        oab, cst, lst, hst = _mixer_prompt(zc, zl, p, nbatch, seq)
        madd = _select_prompt(rel_bias, _block_diag_means(kmean, nbatch, nblk, nbp), qtb, nbp)
        oc = _attn_prompt(qtb, kb, vtb, madd, t0, t1)
        x1 = _outproj(oab, oc, xp, p["w_out"], p["ln1_g"], p["ln1_b"])
        xp, fst = _ffn_prompt(x1, p, nbatch, seq)
        to_heads = lambda a: a.reshape(nbatch, ATTN_HEADS, HEAD_DIM, seq).transpose(0, 3, 1, 2)
        outs["k_p"].append(to_heads(kt))
        outs["v_p"].append(to_heads(vt))
        outs["conv_p"].append(cst)
        outs["lruc_p"].append(lst)
        outs["lruh_p"].append(hst.reshape(nbatch, LRU_CH))
        outs["ffn_p"].append(fst)

        zc, zl, q, k, v = _inproj_sample(xs, p["w_in"])
        oab, cst, lst, hst = _mixer_sample(zc, zl, state_conv[l].transpose(1, 0, 2),
                                           state_lru_conv[l].transpose(1, 0, 2), state_lru_h[l], p)
        q3, k3, v3 = (a.reshape(ns, 1, ATTN_WIDTH) for a in (q, k, v))
        sel = _kv_select(page_table, cache_kt, q3, l)
        sel = sel[:, :, :, 0].transpose(0, 2, 1).reshape(ns, ATTN_HEADS * MOBA_TOPK)
        oc = _attn_sample(page_table, sel, rel_bias, cache_kt, cache_vt, q3, k3, v3, bs, l)
        x1 = _outproj(oab, oc.reshape(ns, ATTN_WIDTH), xs, p["w_out"], p["ln1_g"], p["ln1_b"])
        xs, fst = _ffn_sample(x1, state_ffn_conv[l].transpose(1, 0, 2), p)
        outs["k_s"].append(k.reshape(ns, 1, ATTN_HEADS, HEAD_DIM))
        outs["v_s"].append(v.reshape(ns, 1, ATTN_HEADS, HEAD_DIM))
        outs["conv_s"].append(cst.transpose(1, 0, 2))
        outs["lruc_s"].append(lst.transpose(1, 0, 2))
        outs["lruh_s"].append(hst)
        outs["ffn_s"].append(fst.transpose(1, 0, 2))

    st = {k: jnp.stack(v) for k, v in outs.items()}
    return (xp.reshape(nbatch, seq, D_MODEL), xs.reshape(ns, 1, D_MODEL),
            st["k_p"], st["v_p"], st["k_s"], st["v_s"], st["conv_p"], st["conv_s"],
            st["lruc_p"], st["lruc_s"], st["lruh_p"], st["lruh_s"], st["ffn_p"], st["ffn_s"])
```

```python
import functools
import math

import jax
import jax.numpy as jnp
import numpy as np
from jax import lax
from jax.experimental import pallas as pl
from jax.experimental.pallas import tpu as pltpu

F32 = jnp.float32
BF16 = jnp.bfloat16
I32 = jnp.int32

D_MODEL = 1024
HEAD_DIM = 64
CONV_CH = 256
CONV_WIDTH = 31
LRU_CH = 256
LRU_CONV_WIDTH = 4
LRU_C = 8.0
ATTN_WIDTH = 512
ATTN_HEADS = 8
HEAD_PAIRS = ATTN_HEADS // 2
MOBA_BLOCK = 256
MOBA_TOPK = 3
REL_BUCKETS = 32
REL_MAX_DIST = 128
D_FF = 3 * D_MODEL
FFN_CONV_WIDTH = 3
PAGE_SIZE = 128
PAGES_PER_BLOCK = MOBA_BLOCK // PAGE_SIZE
DEPTH = 4
DEEPNORM_ALPHA = (2 * DEPTH) ** 0.25
LN_EPS = 1e-5
NEG_INF = -1e30
Q_SCALE = HEAD_DIM ** -0.5
LOG2E = math.log2(math.e)

LANES = 128
SUBLANES = 8
BF16_ROWS = 16
VT_ROWS = HEAD_DIM + BF16_ROWS

ROW_TILE = 512
FFN_ROW_TILE = 1024
FF_TILE = 1024
CONV_CHUNK = 64
CONV_HALO = 32
LRU_HALO = 8
KV_CHUNK_PAGES = 16
KV_SLOTS = 3
FAR_GROUP = 4


def _sigmoid(x):
    return 1.0 / (1.0 + jnp.exp(-x))


def _gelu_tanh(x):
    c = math.sqrt(2.0 / math.pi)
    return x * (0.5 * (1.0 + jnp.tanh(c * (x + 0.044715 * (x * x * x)))))


def _layer_norm(x, g, b):
    mu = jnp.mean(x, axis=-1, keepdims=True)
    xc = x - mu
    var = jnp.mean(xc * xc, axis=-1, keepdims=True)
    return xc * lax.rsqrt(var + LN_EPS) * g + b


def _lru_coeffs(xr, wa_ref, ba_ref, wx_ref, bx_ref, lam_ref):
    xb = xr.astype(BF16)
    r = _sigmoid(jnp.dot(xb, wa_ref[...], preferred_element_type=F32) + ba_ref[...])
    i = _sigmoid(jnp.dot(xb, wx_ref[...], preferred_element_type=F32) + bx_ref[...])
    nl = -lam_ref[...]
    softplus = jnp.maximum(nl, 0.0) + jnp.log1p(jnp.exp(-jnp.abs(nl)))
    log_a = (-LRU_C) * r * softplus
    a = jnp.exp(log_a)
    u = jnp.sqrt(-jnp.tanh(log_a) * (a * a + 1.0)) * (i * xr)
    return a, u


def _rel_bucket(dist):
    n = jnp.maximum(dist, 0)
    max_exact = REL_BUCKETS // 2
    nf = jnp.maximum(n, 1).astype(F32)
    large = max_exact + (jnp.log(nf / max_exact) / math.log(REL_MAX_DIST / max_exact)
                         * (REL_BUCKETS - max_exact)).astype(I32)
    return jnp.where(n < max_exact, n, jnp.minimum(large, REL_BUCKETS - 1))


def _bias_lookup(rb_ref, bucket, h):
    val = jnp.zeros(bucket.shape, F32)
    for k in range(REL_BUCKETS):
        val = jnp.where(bucket == k, rb_ref[k, h], val)
    return val


def _bias_tables_kernel(rb_ref, t0_ref, t1_ref, bs_ref):
    h = pl.program_id(0)
    kk = lax.broadcasted_iota(I32, (MOBA_BLOCK, MOBA_BLOCK), 0)
    qq = lax.broadcasted_iota(I32, (MOBA_BLOCK, MOBA_BLOCK), 1)
    d0 = qq - kk
    t0_ref[0] = jnp.where(d0 >= 0, _bias_lookup(rb_ref, _rel_bucket(d0), h) * LOG2E, NEG_INF)
    t1_ref[0] = _bias_lookup(rb_ref, _rel_bucket(d0 + MOBA_BLOCK), h) * LOG2E
    ds = MOBA_BLOCK - lax.broadcasted_iota(I32, (SUBLANES, MOBA_BLOCK), 1)
    bs_ref[0] = _bias_lookup(rb_ref, _rel_bucket(ds), h)


def _bias_tables(rel_bias):
    tbl = jax.ShapeDtypeStruct((ATTN_HEADS, MOBA_BLOCK, MOBA_BLOCK), F32)
    return pl.pallas_call(
        _bias_tables_kernel,
        grid=(ATTN_HEADS,),
        in_specs=[pl.BlockSpec(memory_space=pltpu.SMEM)],
        out_specs=[pl.BlockSpec((1, MOBA_BLOCK, MOBA_BLOCK), lambda h: (h, 0, 0)),
                   pl.BlockSpec((1, MOBA_BLOCK, MOBA_BLOCK), lambda h: (h, 0, 0)),
                   pl.BlockSpec((1, SUBLANES, MOBA_BLOCK), lambda h: (h, 0, 0))],
        out_shape=[tbl, tbl, jax.ShapeDtypeStruct((ATTN_HEADS, SUBLANES, MOBA_BLOCK), F32)],
        name="bias_tables",
    )(rel_bias)


def _inproj_prompt_kernel(x_ref, w_ref, kt_all_ref, vt_all_ref,
                          zc_ref, zl_ref, kt_ref, vt_ref, qtb_ref, kb_ref, vtb_ref, km_ref):
    del kt_all_ref, vt_all_ref
    xb = x_ref[...].astype(BF16)
    tm = xb.shape[0]

    def col(c):
        return jnp.dot(xb, w_ref[0, :, c * ATTN_WIDTH:(c + 1) * ATTN_WIDTH], preferred_element_type=F32)

    zc_ref[...] = col(0)
    zl_ref[...] = col(1)
    qtb_ref[0] = (col(2) * (Q_SCALE * LOG2E)).T.astype(BF16)
    k = col(3)
    kt_ref[0, 0] = k.T
    kb_ref[...] = k.astype(BF16)
    for r in range(tm // MOBA_BLOCK):
        km_ref[r] = jnp.sum(k[r * MOBA_BLOCK:(r + 1) * MOBA_BLOCK], axis=0, keepdims=True) * (1.0 / MOBA_BLOCK)
    vt = col(4).T
    vt_ref[0, 0] = vt
    ones = jnp.ones((VT_ROWS - HEAD_DIM, tm), BF16)
    for h in range(ATTN_HEADS):
        vtb_ref[0, h, 0:HEAD_DIM, :] = vt[h * HEAD_DIM:(h + 1) * HEAD_DIM].astype(BF16)
        vtb_ref[0, h, HEAD_DIM:VT_ROWS, :] = ones


def _inproj_prompt(x, w, kt_all, vt_all, layer, nbatch, seq):
    m = x.shape[0]
    tm = ROW_TILE
    nb = seq // tm
    rows = lambda i: (i, 0)
    tcols = lambda i: (i // nb, 0, i % nb)
    lcols = lambda i: (layer, i // nb, 0, i % nb)
    return pl.pallas_call(
        _inproj_prompt_kernel,
        grid=(m // tm,),
        in_specs=[pl.BlockSpec((tm, D_MODEL), rows),
                  pl.BlockSpec((1,) + w.shape[1:], lambda i: (layer, 0, 0)),
                  pl.BlockSpec(memory_space=pl.ANY),
                  pl.BlockSpec(memory_space=pl.ANY)],
        out_specs=[pl.BlockSpec((tm, ATTN_WIDTH), rows),
                   pl.BlockSpec((tm, ATTN_WIDTH), rows),
                   pl.BlockSpec((1, 1, ATTN_WIDTH, tm), lcols),
                   pl.BlockSpec((1, 1, ATTN_WIDTH, tm), lcols),
                   pl.BlockSpec((1, ATTN_WIDTH, tm), tcols),
                   pl.BlockSpec((tm, ATTN_WIDTH), rows),
                   pl.BlockSpec((1, ATTN_HEADS, VT_ROWS, tm), lambda i: (i // nb, 0, 0, i % nb)),
                   pl.BlockSpec((tm // MOBA_BLOCK, 1, ATTN_WIDTH), lambda i: (i, 0, 0))],
        out_shape=[jax.ShapeDtypeStruct((m, ATTN_WIDTH), F32),
                   jax.ShapeDtypeStruct((m, ATTN_WIDTH), F32),
                   jax.ShapeDtypeStruct(kt_all.shape, F32),
                   jax.ShapeDtypeStruct(vt_all.shape, F32),
                   jax.ShapeDtypeStruct((nbatch, ATTN_WIDTH, seq), BF16),
                   jax.ShapeDtypeStruct((m, ATTN_WIDTH), BF16),
                   jax.ShapeDtypeStruct((nbatch, ATTN_HEADS, VT_ROWS, seq), BF16),
                   jax.ShapeDtypeStruct((m // MOBA_BLOCK, 1, ATTN_WIDTH), F32)],
        input_output_aliases={2: 2, 3: 3},
        compiler_params=pltpu.CompilerParams(dimension_semantics=("parallel",)),
        name="inproj_prompt",
    )(x, w, kt_all, vt_all)


def _inproj_sample_kernel(x_ref, w_ref, zc_ref, zl_ref, q_ref, k_ref, v_ref):
    xb = x_ref[...].astype(BF16)
    for c, o_ref in enumerate((zc_ref, zl_ref, q_ref, k_ref, v_ref)):
        o_ref[...] = jnp.dot(xb, w_ref[0, :, c * ATTN_WIDTH:(c + 1) * ATTN_WIDTH], preferred_element_type=F32)


def _inproj_sample(x, w, layer):
    m = x.shape[0]
    out = jax.ShapeDtypeStruct((m, ATTN_WIDTH), F32)
    return pl.pallas_call(
        _inproj_sample_kernel,
        grid=(1,),
        in_specs=[pl.BlockSpec(x.shape, lambda i: (0, 0)),
                  pl.BlockSpec((1,) + w.shape[1:], lambda i: (layer, 0, 0))],
        out_specs=[pl.BlockSpec((m, ATTN_WIDTH), lambda i: (0, 0))] * 5,
        out_shape=[out] * 5,
        name="inproj_sample",
    )(x, w)


def _mixer_prompt_kernel(zc_ref, zl_ref, cw_ref, cb_ref, lg_ref, lb_ref, lw_ref, lcb_ref,
                         wa_ref, ba_ref, wx_ref, bx_ref, lam_ref,
                         oab_ref, cst_ref, lst_ref, hst_ref,
                         glu_ext, glu_sh, xl_ext, h_sc):
    i = pl.program_id(1)
    tc = zc_ref.shape[0]

    @pl.when(i == 0)
    def _():
        glu_ext[0:CONV_HALO, :] = jnp.zeros((CONV_HALO, CONV_CH), F32)
        xl_ext[0:LRU_HALO, :] = jnp.zeros((LRU_HALO, LRU_CH), F32)
        h_sc[...] = jnp.zeros(h_sc.shape, F32)

    glu_ext[CONV_HALO:CONV_HALO + tc, :] = zc_ref[:, 0:CONV_CH] * _sigmoid(zc_ref[:, CONV_CH:2 * CONV_CH])
    first = CONV_HALO - (CONV_WIDTH - 1)
    span = tc + CONV_HALO - SUBLANES
    for r in range(1, SUBLANES):
        for k0 in range(0, span, CONV_CHUNK):
            n = min(CONV_CHUNK, span - k0)
            glu_sh[r - 1, k0:k0 + n, :] = glu_ext[pl.ds(k0 + r, n), :]
    for c in range(tc // CONV_CHUNK):
        acc = jnp.broadcast_to(cb_ref[...], (CONV_CHUNK, CONV_CH))
        for j in range(CONV_WIDTH):
            r = (first + j) % SUBLANES
            row0 = c * CONV_CHUNK + first + j - r
            tap = glu_ext[row0:row0 + CONV_CHUNK, :] if r == 0 else glu_sh[r - 1, row0:row0 + CONV_CHUNK, :]
            acc = acc + cw_ref[j:j + 1, :] * tap
        ln = _layer_norm(acc, lg_ref[...], lb_ref[...])
        oab_ref[c * CONV_CHUNK:(c + 1) * CONV_CHUNK, 0:CONV_CH] = ln * _sigmoid(ln)

    xl_ext[LRU_HALO:LRU_HALO + tc, :] = zl_ref[:, LRU_CH:2 * LRU_CH]
    lfirst = LRU_HALO - (LRU_CONV_WIDTH - 1)
    xr = jnp.broadcast_to(lcb_ref[...], (tc, LRU_CH))
    for j in range(LRU_CONV_WIDTH):
        xr = xr + lw_ref[j:j + 1, :] * xl_ext[pl.ds(lfirst + j, tc), :]
    a, u = _lru_coeffs(xr, wa_ref, ba_ref, wx_ref, bx_ref, lam_ref)
    row = lax.broadcasted_iota(I32, (tc, LRU_CH), 0)
    s = 1
    while s < tc:
        keep = row >= s
        u = jnp.where(keep, a * pltpu.roll(u, s, 0) + u, u)
        a = jnp.where(keep, a * pltpu.roll(a, s, 0), a)
        s *= 2
    h = a * h_sc[0:1, :] + u
    oab_ref[:, CONV_CH:CONV_CH + LRU_CH] = _gelu_tanh(zl_ref[:, 0:LRU_CH]) * h
    h_last = h[tc - 1:tc, :]

    @pl.when(i == pl.num_programs(1) - 1)
    def _():
        cst_ref[0] = glu_ext[pl.ds(tc + first, CONV_WIDTH - 1), :]
        lst_ref[0] = xl_ext[pl.ds(tc + lfirst, LRU_CONV_WIDTH - 1), :]
        hst_ref[0] = h_last

    glu_ext[0:CONV_HALO, :] = glu_ext[tc:tc + CONV_HALO, :]
    xl_ext[0:LRU_HALO, :] = xl_ext[tc:tc + LRU_HALO, :]
    h_sc[...] = jnp.broadcast_to(h_last, h_sc.shape)


def _mixer_prompt(zc, zl, p, nbatch, seq):
    tc = ROW_TILE
    nt = seq // tc
    rows = lambda b, i: (b * nt + i, 0)
    full = lambda a: pl.BlockSpec(a.shape, lambda b, i: (0,) * a.ndim)
    params = [p["conv_dw_w"], p["conv_dw_b"], p["conv_ln_g"], p["conv_ln_b"], p["lru_conv_w"], p["lru_conv_b"],
              p["wa"], p["lru_ba"], p["wx"], p["lru_bx"], p["lru_lambda"]]
    return pl.pallas_call(
        _mixer_prompt_kernel,
        grid=(nbatch, nt),
        in_specs=[pl.BlockSpec((tc, 2 * CONV_CH), rows), pl.BlockSpec((tc, 2 * LRU_CH), rows)]
        + [full(a) for a in params],
        out_specs=[pl.BlockSpec((tc, CONV_CH + LRU_CH), rows),
                   pl.BlockSpec((1, CONV_WIDTH - 1, CONV_CH), lambda b, i: (b, 0, 0)),
                   pl.BlockSpec((1, LRU_CONV_WIDTH - 1, LRU_CH), lambda b, i: (b, 0, 0)),
                   pl.BlockSpec((1, 1, LRU_CH), lambda b, i: (b, 0, 0))],
        out_shape=[jax.ShapeDtypeStruct((nbatch * seq, CONV_CH + LRU_CH), F32),
                   jax.ShapeDtypeStruct((nbatch, CONV_WIDTH - 1, CONV_CH), F32),
                   jax.ShapeDtypeStruct((nbatch, LRU_CONV_WIDTH - 1, LRU_CH), F32),
                   jax.ShapeDtypeStruct((nbatch, 1, LRU_CH), F32)],
        scratch_shapes=[pltpu.VMEM((tc + CONV_HALO, CONV_CH), F32),
                        pltpu.VMEM((SUBLANES - 1, tc + CONV_HALO - SUBLANES, CONV_CH), F32),
                        pltpu.VMEM((tc + LRU_HALO, LRU_CH), F32),
                        pltpu.VMEM((SUBLANES, LRU_CH), F32)],
        compiler_params=pltpu.CompilerParams(dimension_semantics=("parallel", "arbitrary")),
        name="mixer_prompt",
    )(zc, zl, *params)


def _mixer_sample_kernel(zc_ref, zl_ref, cs_ref, ls_ref, h0_ref, cw_ref, cb_ref, lg_ref, lb_ref, lw_ref, lcb_ref,
                         wa_ref, ba_ref, wx_ref, bx_ref, lam_ref,
                         oab_ref, cst_ref, lst_ref, hst_ref):
    glu = zc_ref[:, 0:CONV_CH] * _sigmoid(zc_ref[:, CONV_CH:2 * CONV_CH])
    acc = cb_ref[...] + cw_ref[CONV_WIDTH - 1:CONV_WIDTH, :] * glu
    for j in range(CONV_WIDTH - 1):
        acc = acc + cw_ref[j:j + 1, :] * cs_ref[j]
        if j >= 1:
            cst_ref[j - 1] = cs_ref[j]
    cst_ref[CONV_WIDTH - 2] = glu
    ln = _layer_norm(acc, lg_ref[...], lb_ref[...])
    oab_ref[:, 0:CONV_CH] = ln * _sigmoid(ln)

    xl = zl_ref[:, LRU_CH:2 * LRU_CH]
    xr = lcb_ref[...] + lw_ref[LRU_CONV_WIDTH - 1:LRU_CONV_WIDTH, :] * xl
    for j in range(LRU_CONV_WIDTH - 1):
        xr = xr + lw_ref[j:j + 1, :] * ls_ref[j]
        if j >= 1:
            lst_ref[j - 1] = ls_ref[j]
    lst_ref[LRU_CONV_WIDTH - 2] = xl
    a, u = _lru_coeffs(xr, wa_ref, ba_ref, wx_ref, bx_ref, lam_ref)
    h = a * h0_ref[...] + u
    hst_ref[...] = h
    oab_ref[:, CONV_CH:CONV_CH + LRU_CH] = _gelu_tanh(zl_ref[:, 0:LRU_CH]) * h


def _mixer_sample(zc, zl, conv_state, lru_state, h0, p):
    ns = zc.shape[0]
    params = [p["conv_dw_w"], p["conv_dw_b"], p["conv_ln_g"], p["conv_ln_b"], p["lru_conv_w"], p["lru_conv_b"],
              p["wa"], p["lru_ba"], p["wx"], p["lru_bx"], p["lru_lambda"]]
    return pl.pallas_call(
        _mixer_sample_kernel,
        out_shape=[jax.ShapeDtypeStruct((ns, CONV_CH + LRU_CH), F32),
                   jax.ShapeDtypeStruct(conv_state.shape, F32),
                   jax.ShapeDtypeStruct(lru_state.shape, F32),
                   jax.ShapeDtypeStruct((ns, LRU_CH), F32)],
        name="mixer_sample",
    )(zc, zl, conv_state, lru_state, h0, *params)


def _select_prompt_kernel(rb_ref, km_ref, qt_ref, madd_ref):
    i = pl.program_id(1)
    nbp = madd_ref.shape[2] - SUBLANES
    gate = jnp.dot(km_ref[0], qt_ref[0].astype(F32), preferred_element_type=F32,
                   precision=lax.Precision.HIGHEST)
    n = lax.broadcasted_iota(I32, (nbp, MOBA_BLOCK), 0)
    nf = n.astype(F32)
    past = n < i
    tail_row = lax.broadcasted_iota(I32, (SUBLANES, MOBA_BLOCK), 0)
    for h in range(ATTN_HEADS):
        g = jnp.where(past, gate[h * nbp:(h + 1) * nbp], NEG_INF)
        sel = jnp.zeros((nbp, MOBA_BLOCK), jnp.bool_)
        for _ in range(MOBA_TOPK):
            best = jnp.max(g, axis=0, keepdims=True)
            idx = jnp.min(jnp.where(g == best, nf, float(nbp)), axis=0, keepdims=True)
            pick = nf == idx
            sel = jnp.logical_or(sel, pick)
            g = jnp.where(pick, -jnp.inf, g)
        far_bias = rb_ref[REL_BUCKETS - 1, h] * LOG2E
        madd_ref[0, h, 0:nbp, :] = jnp.where(jnp.logical_and(sel, n < i - 1), far_bias, NEG_INF)
        prev = jnp.max(jnp.where(jnp.logical_and(sel, n == i - 1), 0.0, NEG_INF), axis=0, keepdims=True)
        madd_ref[0, h, nbp:nbp + SUBLANES, :] = jnp.where(tail_row == 0, prev, NEG_INF)


def _select_prompt(rel_bias, kmbd, qtb, nbp):
    nbatch, _, seq = qtb.shape
    nblk = seq // MOBA_BLOCK
    rows = nbp + SUBLANES
    return pl.pallas_call(
        _select_prompt_kernel,
        grid=(nbatch, nblk),
        in_specs=[pl.BlockSpec(memory_space=pltpu.SMEM),
                  pl.BlockSpec((1, ATTN_HEADS * nbp, ATTN_WIDTH), lambda b, i: (b, 0, 0)),
                  pl.BlockSpec((1, ATTN_WIDTH, MOBA_BLOCK), lambda b, i: (b, 0, i))],
        out_specs=pl.BlockSpec((1, ATTN_HEADS, rows, MOBA_BLOCK), lambda b, i: (b, 0, 0, i)),
        out_shape=jax.ShapeDtypeStruct((nbatch, ATTN_HEADS, rows, seq), F32),
        compiler_params=pltpu.CompilerParams(dimension_semantics=("parallel", "arbitrary")),
        name="select_prompt",
    )(rel_bias, kmbd, qtb)


def _far_schedule(nblk):
    qblk, base, first = [], [], []
    for i in range(nblk):
        for g in range((i + FAR_GROUP - 2) // FAR_GROUP):
            qblk.append(i)
            base.append(g * FAR_GROUP)
            first.append(1 if g == 0 else 0)
    return np.array([qblk, base, first], np.int32)


def _attn_prompt_kernel(sched_ref, qt_ref, k_ref, vt_ref, madd_ref, t0_ref, t1_ref, o_ref,
                        sa_sc, sb_sc, pa_sc, pb_sc, mn_sc, acc_sc):
    nblk = qt_ref.shape[2] // MOBA_BLOCK
    nbp = madd_ref.shape[2] - SUBLANES
    ntile = sched_ref.shape[1]
    width = 2 * MOBA_BLOCK
    near = 2 * MOBA_BLOCK
    group = FAR_GROUP * MOBA_BLOCK

    def qcols(i):
        return pl.ds(pl.multiple_of(i * MOBA_BLOCK, MOBA_BLOCK), MOBA_BLOCK)

    def both_heads(i):
        qt = qt_ref[0, :, qcols(i)]
        z = jnp.zeros((HEAD_DIM, MOBA_BLOCK), BF16)
        return jnp.concatenate([jnp.concatenate([qt[0:HEAD_DIM], z], axis=1),
                                jnp.concatenate([z, qt[HEAD_DIM:2 * HEAD_DIM]], axis=1)], axis=0)

    def lanes2(fn):
        return jnp.concatenate([fn(0), fn(1)], axis=1)

    def pv(p_ref, rows0, nrows):
        return jnp.stack([jnp.dot(vt_ref[0, e, :, pl.ds(rows0, nrows)],
                                  p_ref[0:nrows, e * MOBA_BLOCK:(e + 1) * MOBA_BLOCK],
                                  preferred_element_type=F32) for e in range(2)])

    buf_a, buf_b = (sa_sc, pa_sc), (sb_sc, pb_sc)

    def near_rows(i):
        return pl.multiple_of(jnp.maximum(i - 1, 0) * MOBA_BLOCK, MOBA_BLOCK)

    def near_x(i, buf, first=False):
        s_ref, _ = buf
        r0 = 0 if first else near_rows(i)
        qb = both_heads(i)
        cmax = None
        for c in range(2):
            s = jnp.dot(k_ref[pl.ds(r0 + c * MOBA_BLOCK, MOBA_BLOCK), :], qb, preferred_element_type=F32)
            if first:
                s = s + (lanes2(lambda e: t0_ref[e]) if c == 0 else NEG_INF)
            elif c == 0:
                s = s + (lanes2(lambda e: t1_ref[e]) + lanes2(lambda e: madd_ref[0, e, nbp:nbp + 1, qcols(i)]))
            else:
                s = s + lanes2(lambda e: t0_ref[e])
            s_ref[c * MOBA_BLOCK:(c + 1) * MOBA_BLOCK, :] = s
            bmax = jnp.max(s, axis=0, keepdims=True)
            cmax = bmax if cmax is None else jnp.maximum(cmax, bmax)
        return cmax

    def near_y(i, buf, m):
        s_ref, p_ref = buf
        for c in range(2):
            rows = slice(c * MOBA_BLOCK, (c + 1) * MOBA_BLOCK)
            p_ref[rows, :] = jnp.exp2(s_ref[rows, :] - m).astype(BF16)
        mn_sc[pl.ds(i, 1), :] = m

    def near_pv(i, buf):
        acc_sc[i] = pv(buf[1], near_rows(i), near)

    m_even = near_x(0, buf_a, first=True)
    m_odd = near_x(1, buf_b)
    near_y(0, buf_a, m_even)

    def near_body(h, m_odd):
        i = 2 * h + 1
        near_pv(i - 1, buf_a)
        m_even = near_x(i + 1, buf_a)
        near_y(i, buf_b, m_odd)
        near_pv(i, buf_b)
        m_odd = near_x(i + 2, buf_b)
        near_y(i + 1, buf_a, m_even)
        return m_odd

    m_odd = lax.fori_loop(0, nblk // 2 - 1, near_body, m_odd)
    near_pv(nblk - 2, buf_a)
    near_y(nblk - 1, buf_b, m_odd)
    near_pv(nblk - 1, buf_b)

    def tile(n):
        return sched_ref[0, n], sched_ref[1, n], sched_ref[2, n]

    def far_rows(i, base):
        return [lanes2(lambda e: madd_ref[0, e, pl.ds(base + c, 1), qcols(i)]) for c in range(FAR_GROUP)]

    def far_x(n, buf):
        s_ref, _ = buf
        i, base, _ = tile(n)
        qb = both_heads(i)

        def block(c):
            kc = k_ref[pl.ds(pl.multiple_of((base + c) * MOBA_BLOCK, MOBA_BLOCK), MOBA_BLOCK), :]
            s = jnp.dot(kc, qb, preferred_element_type=F32)
            s_ref[c * MOBA_BLOCK:(c + 1) * MOBA_BLOCK, :] = s
            return jnp.max(s, axis=0, keepdims=True)
        return block

    def far_state(n, bmax, m_prev):
        i, base, first = tile(n)
        m_in = jnp.where(first == 1, mn_sc[pl.ds(i, 1), :], m_prev)
        add = far_rows(i, base)
        m_new = m_in
        for c in range(FAR_GROUP):
            m_new = jnp.maximum(m_new, bmax[c] + add[c])
        return m_in, m_new

    def far_y(n, buf, m_new, x_next=None):
        s_ref, p_ref = buf
        i, base, _ = tile(n)
        add = far_rows(i, base)
        nxt = []
        for c in range(FAR_GROUP):
            rows = slice(c * MOBA_BLOCK, (c + 1) * MOBA_BLOCK)
            shift = m_new - add[c]
            if x_next is not None and c >= 1:
                shift = shift + 0.0 * nxt[c - 1]
            p_ref[rows, :] = jnp.exp2(s_ref[rows, :] - shift).astype(BF16)
            if x_next is not None:
                nxt.append(x_next(c))
        return nxt

    def far_pv(n, buf, alpha):
        i, base, _ = tile(n)
        scale = jnp.stack([jnp.broadcast_to(alpha[:, e * MOBA_BLOCK:(e + 1) * MOBA_BLOCK], (VT_ROWS, MOBA_BLOCK))
                           for e in range(2)])
        acc_sc[i] = scale * acc_sc[i] + pv(buf[1], pl.multiple_of(base * MOBA_BLOCK, group), group)

    if ntile:
        assert ntile % 2 == 0
        first_block = far_x(0, buf_a)
        m_in, m_new = far_state(0, [first_block(c) for c in range(FAR_GROUP)], jnp.zeros((1, width), F32))

        def step(n, cur, nxt, carry, has_prev=True, has_next=True):
            alpha_prev, m_in, m_new = carry
            bmax = far_y(n, cur, m_new, far_x(n + 1, nxt) if has_next else None)
            if has_prev:
                far_pv(n - 1, nxt, alpha_prev)
            alpha = jnp.exp2(m_in - m_new)
            if not has_next:
                return alpha
            return (alpha,) + far_state(n + 1, bmax, m_new)

        carry = step(0, buf_a, buf_b, (None, m_in, m_new), has_prev=False)

        def far_body(h, carry):
            carry = step(2 * h + 1, buf_b, buf_a, carry)
            return step(2 * h + 2, buf_a, buf_b, carry)

        carry = lax.fori_loop(0, ntile // 2 - 1, far_body, carry)
        alpha = step(ntile - 1, buf_b, buf_a, carry, has_next=False)
        far_pv(ntile - 1, buf_b, alpha)

    def finish(i, _):
        acc = acc_sc[i]
        ot = jnp.concatenate([acc[e, 0:HEAD_DIM] / acc[e, HEAD_DIM:HEAD_DIM + 1] for e in range(2)], axis=0)
        o_ref[pl.ds(pl.multiple_of(i * MOBA_BLOCK, MOBA_BLOCK), MOBA_BLOCK), :] = ot.T
        return 0

    lax.fori_loop(0, nblk, finish, 0)


def _attn_prompt(qtb, kb, vtb, madd, t0, t1):
    nbatch, _, seq = qtb.shape
    nblk = seq // MOBA_BLOCK
    rows = madd.shape[2]
    width = 2 * MOBA_BLOCK
    sched = _far_schedule(nblk)
    return pl.pallas_call(
        _attn_prompt_kernel,
        grid_spec=pltpu.PrefetchScalarGridSpec(
            num_scalar_prefetch=1,
            grid=(nbatch, HEAD_PAIRS),
            in_specs=[pl.BlockSpec((1, 2 * HEAD_DIM, seq), lambda b, p, s: (b, p, 0)),
                      pl.BlockSpec((seq, 2 * HEAD_DIM), lambda b, p, s: (b, p)),
                      pl.BlockSpec((1, 2, VT_ROWS, seq), lambda b, p, s: (b, p, 0, 0)),
                      pl.BlockSpec((1, 2, rows, seq), lambda b, p, s: (b, p, 0, 0)),
                      pl.BlockSpec((2, MOBA_BLOCK, MOBA_BLOCK), lambda b, p, s: (p, 0, 0)),
                      pl.BlockSpec((2, MOBA_BLOCK, MOBA_BLOCK), lambda b, p, s: (p, 0, 0))],
            out_specs=pl.BlockSpec((seq, 2 * HEAD_DIM), lambda b, p, s: (b, p)),
            scratch_shapes=[pltpu.VMEM((FAR_GROUP * MOBA_BLOCK, width), F32),
                            pltpu.VMEM((FAR_GROUP * MOBA_BLOCK, width), F32),
                            pltpu.VMEM((FAR_GROUP * MOBA_BLOCK, width), BF16),
                            pltpu.VMEM((FAR_GROUP * MOBA_BLOCK, width), BF16),
                            pltpu.VMEM((nblk, width), F32),
                            pltpu.VMEM((nblk, 2, VT_ROWS, MOBA_BLOCK), F32)]),
        out_shape=jax.ShapeDtypeStruct((nbatch * seq, ATTN_WIDTH), F32),
        compiler_params=pltpu.CompilerParams(dimension_semantics=("parallel", "parallel")),
        name="attn_prompt",
    )(jnp.asarray(sched), qtb, kb, vtb, madd, t0, t1)


def _make_kv_select_kernel(layer, n_pages):
    nchunk = n_pages // KV_CHUNK_PAGES
    blocks_per_chunk = KV_CHUNK_PAGES // PAGES_PER_BLOCK

    def page_copy(pt_ref, cache_ref, buf, sem, b, c, slot, p):
        page = pt_ref[b, c * KV_CHUNK_PAGES + p]
        return pltpu.make_async_copy(cache_ref.at[layer, page], buf.at[slot, p], sem.at[slot])

    def kernel(pt_ref, cache_ref, q_ref, sel_ref, buf, sem, kmt):
        b = pl.program_id(0)
        c = pl.program_id(1)
        step = b * nchunk + c
        nsteps = pl.num_programs(0) * nchunk
        slot = step % KV_SLOTS
        ahead = KV_SLOTS - 1

        def start(s):
            for p in range(KV_CHUNK_PAGES):
                page_copy(pt_ref, cache_ref, buf, sem, s // nchunk, s % nchunk, s % KV_SLOTS, p).start()

        @pl.when(step == 0)
        def _():
            for s in range(ahead):
                start(s)

        @pl.when(step + ahead < nsteps)
        def _():
            start(step + ahead)

        for p in range(KV_CHUNK_PAGES):
            page_copy(pt_ref, cache_ref, buf, sem, b, c, slot, p).wait()

        @pl.when(c == 0)
        def _():
            kmt[...] = jnp.zeros(kmt.shape, F32)

        lane = lax.broadcasted_iota(I32, kmt.shape, 1)
        km = kmt[...]
        for kb in range(blocks_per_chunk):
            tot = buf[slot, PAGES_PER_BLOCK * kb]
            for r in range(1, PAGES_PER_BLOCK):
                tot = tot + buf[slot, PAGES_PER_BLOCK * kb + r]
            mean = jnp.sum(tot.reshape(ATTN_WIDTH, PAGE_SIZE), axis=1, keepdims=True) * (1.0 / MOBA_BLOCK)
            km = jnp.where(lane == c * blocks_per_chunk + kb, mean, km)
        kmt[...] = km

        @pl.when(c == nchunk - 1)
        def _():
            nblk = n_pages // PAGES_PER_BLOCK
            hrow = lax.broadcasted_iota(I32, (SUBLANES, ATTN_WIDTH), 0)
            hcol = lax.broadcasted_iota(I32, (SUBLANES, ATTN_WIDTH), 1) // HEAD_DIM
            qbd = jnp.where(hrow == hcol, jnp.broadcast_to(q_ref[0], (SUBLANES, ATTN_WIDTH)), 0.0)
            gate = jnp.dot(qbd, km, preferred_element_type=F32, precision=lax.Precision.HIGHEST)
            nf = lax.broadcasted_iota(I32, gate.shape, 1).astype(F32)
            g = jnp.where(nf < nblk, gate, -jnp.inf)
            for r in range(MOBA_TOPK):
                best = jnp.max(g, axis=1, keepdims=True)
                idx = jnp.min(jnp.where(g == best, nf, float(LANES)), axis=1, keepdims=True)
                sel_ref[0, r] = jnp.broadcast_to(idx, (SUBLANES, LANES)).astype(I32)
                g = jnp.where(nf == idx, -jnp.inf, g)

    return kernel


def _kv_select(page_table, cache_t, q3, layer):
    ns, n_pages = page_table.shape
    nchunk = n_pages // KV_CHUNK_PAGES
    return pl.pallas_call(
        _make_kv_select_kernel(layer, n_pages),
        grid_spec=pltpu.PrefetchScalarGridSpec(
            num_scalar_prefetch=1,
            grid=(ns, nchunk),
            in_specs=[pl.BlockSpec(memory_space=pl.ANY),
                      pl.BlockSpec((1, 1, ATTN_WIDTH), lambda b, c, pt: (b, 0, 0))],
            out_specs=pl.BlockSpec((1, MOBA_TOPK, SUBLANES, LANES), lambda b, c, pt: (b, 0, 0, 0)),
            scratch_shapes=[pltpu.VMEM((KV_SLOTS, KV_CHUNK_PAGES, ATTN_HEADS, HEAD_DIM, PAGE_SIZE), F32),
                            pltpu.SemaphoreType.DMA((KV_SLOTS,)),
                            pltpu.VMEM((ATTN_WIDTH, LANES), F32)]),
        out_shape=jax.ShapeDtypeStruct((ns, MOBA_TOPK, SUBLANES, LANES), I32),
        compiler_params=pltpu.CompilerParams(dimension_semantics=("arbitrary", "arbitrary")),
        name="kv_select",
    )(page_table, cache_t, q3)


def _make_attn_sample_kernel(layer, n_pages):
    nblk = n_pages // PAGES_PER_BLOCK
    nsel = MOBA_TOPK * MOBA_BLOCK

    def copies(pt_ref, sel_ref, ck_ref, cv_ref, kbuf, vbuf, sem, b, slot):
        out = []
        for h in range(ATTN_HEADS):
            for r in range(MOBA_TOPK):
                blk = sel_ref[b, h * MOBA_TOPK + r]
                for half in range(PAGES_PER_BLOCK):
                    page = pt_ref[b, blk * PAGES_PER_BLOCK + half]
                    dst = pl.ds(r * MOBA_BLOCK + half * PAGE_SIZE, PAGE_SIZE)
                    out.append(pltpu.make_async_copy(ck_ref.at[layer, page, h], kbuf.at[slot, h, :, dst], sem.at[0, slot]))
                    out.append(pltpu.make_async_copy(cv_ref.at[layer, page, h], vbuf.at[slot, h, :, dst], sem.at[1, slot]))
        return out

    def kernel(pt_ref, sel_ref, rb_ref, ck_ref, cv_ref, q_ref, kn_ref, vn_ref, bs_ref, o_ref, kbuf, vbuf, sem):
        b = pl.program_id(0)
        slot = b % 2

        @pl.when(b == 0)
        def _():
            for cp in copies(pt_ref, sel_ref, ck_ref, cv_ref, kbuf, vbuf, sem, 0, 0):
                cp.start()

        @pl.when(b + 1 < pl.num_programs(0))
        def _():
            for cp in copies(pt_ref, sel_ref, ck_ref, cv_ref, kbuf, vbuf, sem, b + 1, 1 - slot):
                cp.start()

        for cp in copies(pt_ref, sel_ref, ck_ref, cv_ref, kbuf, vbuf, sem, b, slot):
            cp.wait()

        q = q_ref[0] * Q_SCALE
        kn = kn_ref[0]
        vn = vn_ref[0]
        for h in range(ATTN_HEADS):
            hs = slice(h * HEAD_DIM, (h + 1) * HEAD_DIM)
            qh = q[:, hs]
            q8 = jnp.broadcast_to(qh, (SUBLANES, HEAD_DIM)).astype(BF16)
            s = jnp.dot(q8, kbuf[slot, h].astype(BF16), preferred_element_type=F32)
            far_bias = rb_ref[REL_BUCKETS - 1, h]
            bias = jnp.concatenate(
                [jnp.where(sel_ref[b, h * MOBA_TOPK + r] == nblk - 1, bs_ref[h], far_bias)
                 for r in range(MOBA_TOPK)], axis=1)
            s = s + bias
            s_new = jnp.sum(qh * kn[:, hs], axis=1, keepdims=True) + rb_ref[0, h]
            m = jnp.maximum(jnp.max(s, axis=1, keepdims=True), s_new)
            p = jnp.exp(s - m)
            p_new = jnp.exp(s_new - m)
            l = jnp.sum(p, axis=1, keepdims=True) + p_new
            o = lax.dot_general(p.astype(BF16), vbuf[slot, h].astype(BF16), (((1,), (1,)), ((), ())),
                                preferred_element_type=F32)
            o = (o + p_new * vn[:, hs]) / l
            o_ref[0, :, hs] = o[0:1]

    return kernel


def _attn_sample(page_table, sel, rel_bias, cache_kt, cache_vt, q3, k3, v3, bs, layer):
    ns, n_pages = page_table.shape
    nsel = MOBA_TOPK * MOBA_BLOCK
    row = pl.BlockSpec((1, 1, ATTN_WIDTH), lambda b, pt, sl: (b, 0, 0))
    return pl.pallas_call(
        _make_attn_sample_kernel(layer, n_pages),
        grid_spec=pltpu.PrefetchScalarGridSpec(
            num_scalar_prefetch=2,
            grid=(ns,),
            in_specs=[pl.BlockSpec(memory_space=pltpu.SMEM),
                      pl.BlockSpec(memory_space=pl.ANY),
                      pl.BlockSpec(memory_space=pl.ANY),
                      row, row, row,
                      pl.BlockSpec(bs.shape, lambda b, pt, sl: (0, 0, 0))],
            out_specs=row,
            scratch_shapes=[pltpu.VMEM((2, ATTN_HEADS, HEAD_DIM, nsel), F32),
                            pltpu.VMEM((2, ATTN_HEADS, HEAD_DIM, nsel), F32),
                            pltpu.SemaphoreType.DMA((2, 2))]),
        out_shape=jax.ShapeDtypeStruct((ns, 1, ATTN_WIDTH), F32),
        compiler_params=pltpu.CompilerParams(dimension_semantics=("arbitrary",)),
        name="attn_sample",
    )(page_table, sel, rel_bias, cache_kt, cache_vt, q3, k3, v3, bs)


def _outproj_kernel(ab_ref, c_ref, x_ref, w_ref, g_ref, b_ref, o_ref):
    split = CONV_CH + LRU_CH
    y = jnp.dot(ab_ref[...].astype(BF16), w_ref[0, 0:split, :], preferred_element_type=F32)
    y = y + jnp.dot(c_ref[...].astype(BF16), w_ref[0, split:split + ATTN_WIDTH, :], preferred_element_type=F32)
    o_ref[...] = _layer_norm(DEEPNORM_ALPHA * x_ref[...] + y, g_ref[...], b_ref[...])


def _outproj(oab, oc, x, w, layer, g, b):
    m = x.shape[0]
    tm = min(ROW_TILE, m)
    rows = lambda i: (i, 0)
    const = lambda a: pl.BlockSpec(a.shape, lambda i: (0, 0))
    return pl.pallas_call(
        _outproj_kernel,
        grid=(m // tm,),
        in_specs=[pl.BlockSpec((tm, CONV_CH + LRU_CH), rows), pl.BlockSpec((tm, ATTN_WIDTH), rows),
                  pl.BlockSpec((tm, D_MODEL), rows),
                  pl.BlockSpec((1,) + w.shape[1:], lambda i: (layer, 0, 0)), const(g), const(b)],
        out_specs=pl.BlockSpec((tm, D_MODEL), rows),
        out_shape=jax.ShapeDtypeStruct((m, D_MODEL), F32),
        compiler_params=pltpu.CompilerParams(dimension_semantics=("parallel",)),
        name="outproj",
    )(oab, oc, x, w, g, b)


def _make_ffn_prompt_kernel(tiles_per_seq):
    halo = BF16_ROWS
    taps = FFN_CONV_WIDTH

    def kernel(x_ref, xh_ref, wu_ref, wg_ref, dw_ref, db_ref, wd_ref, g_ref, b_ref, o_ref, fst_ref, xb_sc, up_sc, acc_sc):
        i = pl.program_id(0)
        n = pl.program_id(1)
        tm = x_ref.shape[0]

        @pl.when(n == 0)
        def _():
            keep = jnp.where(i % tiles_per_seq == 0, 0.0, 1.0)
            xb_sc[0:halo, :] = (xh_ref[...] * keep).astype(BF16)
            xb_sc[halo:halo + tm, :] = x_ref[...].astype(BF16)
            acc_sc[...] = jnp.zeros(acc_sc.shape, F32)

        up_sc[...] = jnp.dot(xb_sc[...], wu_ref[0], preferred_element_type=F32)
        y = jnp.broadcast_to(db_ref[...], (tm, up_sc.shape[1]))
        for j in range(taps):
            y = y + dw_ref[j:j + 1, :] * up_sc[pl.ds(halo - (taps - 1) + j, tm), :]
        fst_ref[0] = up_sc[pl.ds(halo + tm - (taps - 1), taps - 1), :]
        gate = jnp.dot(xb_sc[halo:halo + tm, :], wg_ref[0], preferred_element_type=F32)
        acc_sc[...] += jnp.dot((_gelu_tanh(y) * gate).astype(BF16), wd_ref[0], preferred_element_type=F32)

        @pl.when(n == pl.num_programs(1) - 1)
        def _():
            o_ref[...] = _layer_norm(DEEPNORM_ALPHA * x_ref[...] + acc_sc[...], g_ref[...], b_ref[...])

    return kernel


def _ffn_prompt(x, p, nbatch, seq):
    m = x.shape[0]
    tm, tn = FFN_ROW_TILE, FF_TILE
    nt = seq // tm
    halo = BF16_ROWS
    hb = tm // halo
    layer = p["layer"]
    y, tails = pl.pallas_call(
        _make_ffn_prompt_kernel(nt),
        grid=(m // tm, D_FF // tn),
        in_specs=[pl.BlockSpec((tm, D_MODEL), lambda i, n: (i, 0)),
                  pl.BlockSpec((halo, D_MODEL), lambda i, n: (jnp.maximum(i * hb - 1, 0), 0)),
                  pl.BlockSpec((1, D_MODEL, tn), lambda i, n: (layer, 0, n)),
                  pl.BlockSpec((1, D_MODEL, tn), lambda i, n: (layer, 0, n)),
                  pl.BlockSpec((FFN_CONV_WIDTH, tn), lambda i, n: (0, n)),
                  pl.BlockSpec((1, tn), lambda i, n: (0, n)),
                  pl.BlockSpec((1, tn, D_MODEL), lambda i, n: (layer, n, 0)),
                  pl.BlockSpec((1, D_MODEL), lambda i, n: (0, 0)),
                  pl.BlockSpec((1, D_MODEL), lambda i, n: (0, 0))],
        out_specs=[pl.BlockSpec((tm, D_MODEL), lambda i, n: (i, 0)),
                   pl.BlockSpec((1, FFN_CONV_WIDTH - 1, tn), lambda i, n: (i, 0, n))],
        out_shape=[jax.ShapeDtypeStruct((m, D_MODEL), F32),
                   jax.ShapeDtypeStruct((m // tm, FFN_CONV_WIDTH - 1, D_FF), F32)],
        scratch_shapes=[pltpu.VMEM((tm + halo, D_MODEL), BF16),
                        pltpu.VMEM((tm + halo, tn), F32),
                        pltpu.VMEM((tm, D_MODEL), F32)],
        compiler_params=pltpu.CompilerParams(dimension_semantics=("arbitrary", "arbitrary")),
        name="ffn_prompt",
    )(x, x, p["w_up"], p["w_gate"], p["ffn_dw_w"], p["ffn_dw_b"], p["w_down"], p["ln2_g"], p["ln2_b"])
    return y, tails[nt - 1::nt]


def _ffn_sample_kernel(x_ref, st_ref, wu_ref, wg_ref, dw_ref, db_ref, wd_ref, g_ref, b_ref, o_ref, fst_ref, acc_sc):
    n = pl.program_id(0)
    taps = FFN_CONV_WIDTH

    @pl.when(n == 0)
    def _():
        acc_sc[...] = jnp.zeros(acc_sc.shape, F32)

    xb = x_ref[...].astype(BF16)
    up = jnp.dot(xb, wu_ref[0], preferred_element_type=F32)
    y = db_ref[...] + dw_ref[taps - 1:taps, :] * up
    for j in range(taps - 1):
        y = y + dw_ref[j:j + 1, :] * st_ref[j]
        if j >= 1:
            fst_ref[j - 1] = st_ref[j]
    fst_ref[taps - 2] = up
    gate = jnp.dot(xb, wg_ref[0], preferred_element_type=F32)
    acc_sc[...] += jnp.dot((_gelu_tanh(y) * gate).astype(BF16), wd_ref[0], preferred_element_type=F32)

    @pl.when(n == pl.num_programs(0) - 1)
    def _():
        o_ref[...] = _layer_norm(DEEPNORM_ALPHA * x_ref[...] + acc_sc[...], g_ref[...], b_ref[...])


def _ffn_sample(x, state, p):
    ns = x.shape[0]
    tn = FF_TILE
    layer = p["layer"]
    return pl.pallas_call(
        _ffn_sample_kernel,
        grid=(D_FF // tn,),
        in_specs=[pl.BlockSpec((ns, D_MODEL), lambda n: (0, 0)),
                  pl.BlockSpec((FFN_CONV_WIDTH - 1, ns, tn), lambda n: (0, 0, n)),
                  pl.BlockSpec((1, D_MODEL, tn), lambda n: (layer, 0, n)),
                  pl.BlockSpec((1, D_MODEL, tn), lambda n: (layer, 0, n)),
                  pl.BlockSpec((FFN_CONV_WIDTH, tn), lambda n: (0, n)),
                  pl.BlockSpec((1, tn), lambda n: (0, n)),
                  pl.BlockSpec((1, tn, D_MODEL), lambda n: (layer, n, 0)),
                  pl.BlockSpec((1, D_MODEL), lambda n: (0, 0)),
                  pl.BlockSpec((1, D_MODEL), lambda n: (0, 0))],
        out_specs=[pl.BlockSpec((ns, D_MODEL), lambda n: (0, 0)),
                   pl.BlockSpec((FFN_CONV_WIDTH - 1, ns, tn), lambda n: (0, 0, n))],
        out_shape=[jax.ShapeDtypeStruct((ns, D_MODEL), F32),
                   jax.ShapeDtypeStruct((FFN_CONV_WIDTH - 1, ns, D_FF), F32)],
        scratch_shapes=[pltpu.VMEM((ns, D_MODEL), F32)],
        compiler_params=pltpu.CompilerParams(dimension_semantics=("arbitrary",)),
        name="ffn_sample",
    )(x, state, p["w_up"], p["w_gate"], p["ffn_dw_w"], p["ffn_dw_b"], p["w_down"], p["ln2_g"], p["ln2_b"])


def _block_diag(w):
    nb, n, _ = w.shape
    eye = jnp.eye(nb, dtype=w.dtype)
    return (eye[:, None, :, None] * w[:, :, None, :]).reshape(nb * n, nb * n)


def _layer_params(l, big, conv_dw_w, conv_dw_b, conv_ln_g, conv_ln_b, lru_conv_w, lru_conv_b, lru_wa, lru_ba,
                  lru_wx, lru_bx, lru_lambda, ln1_g, ln1_b, ffn_dw_w, ffn_dw_b, ln2_g, ln2_b):
    row = lambda a: a[l][None, :]
    return {
        "layer": l, "w_in": big["w_in"], "w_out": big["w_out"], "w_up": big["w_up"], "w_gate": big["w_gate"],
        "w_down": big["w_down"],
        "conv_dw_w": conv_dw_w[l], "conv_dw_b": row(conv_dw_b),
        "conv_ln_g": row(conv_ln_g), "conv_ln_b": row(conv_ln_b), "lru_conv_w": lru_conv_w[l],
        "lru_conv_b": row(lru_conv_b), "wa": _block_diag(lru_wa[l]).astype(BF16), "lru_ba": row(lru_ba),
        "wx": _block_diag(lru_wx[l]).astype(BF16), "lru_bx": row(lru_bx), "lru_lambda": row(lru_lambda),
        "ln1_g": row(ln1_g), "ln1_b": row(ln1_b), "ffn_dw_w": ffn_dw_w[l],
        "ffn_dw_b": row(ffn_dw_b), "ln2_g": row(ln2_g), "ln2_b": row(ln2_b),
    }


def _block_diag_means(kmean, nbatch, nblk, nbp):
    km = kmean.reshape(nbatch, nblk, ATTN_HEADS, HEAD_DIM).transpose(0, 2, 1, 3)
    km = jnp.pad(km, ((0, 0), (0, 0), (0, nbp - nblk), (0, 0)))
    eye = jnp.eye(ATTN_HEADS, dtype=F32)
    bd = km[:, :, :, None, :] * eye[None, :, None, :, None]
    return bd.reshape(nbatch, ATTN_HEADS * nbp, ATTN_WIDTH)


def kernel(x_prompt, x_sample, cache_k, cache_v, page_table, state_conv, state_lru_conv, state_lru_h, state_ffn_conv, w_in, conv_dw_w, conv_dw_b, conv_ln_g, conv_ln_b, lru_conv_w, lru_conv_b, lru_wa, lru_ba, lru_wx, lru_bx, lru_lambda, rel_bias, w_out, ln1_g, ln1_b, ffn_w_up, ffn_w_gate, ffn_dw_w, ffn_dw_b, ffn_w_down, ln2_g, ln2_b):
    nbatch, seq, _ = x_prompt.shape
    ns = x_sample.shape[0]
    depth = w_in.shape[0]
    nblk = seq // MOBA_BLOCK
    nbp = -(-nblk // SUBLANES) * SUBLANES
    assert seq % (FAR_GROUP * MOBA_BLOCK) == 0 and seq % ROW_TILE == 0 and x_sample.shape[1] == 1
    assert page_table.shape[1] % KV_CHUNK_PAGES == 0 and cache_k.shape[2] == PAGE_SIZE

    t0, t1, bs = _bias_tables(rel_bias)
    cache_kt = cache_k.transpose(0, 1, 3, 4, 2)
    cache_vt = cache_v.transpose(0, 1, 3, 4, 2)

    xp = x_prompt.reshape(nbatch * seq, D_MODEL)
    xs = x_sample.reshape(ns, D_MODEL)
    outs = {k: [] for k in ("k_s", "v_s", "conv_p", "conv_s", "lruc_p", "lruc_s",
                            "lruh_p", "lruh_s", "ffn_p", "ffn_s")}
    kt_all = jnp.zeros((depth, nbatch, ATTN_WIDTH, seq), F32)
    vt_all = jnp.zeros((depth, nbatch, ATTN_WIDTH, seq), F32)
    big = {"w_in": w_in.astype(BF16), "w_out": w_out.astype(BF16), "w_up": ffn_w_up.astype(BF16),
           "w_gate": ffn_w_gate.astype(BF16), "w_down": ffn_w_down.astype(BF16)}
    for l in range(depth):
        p = _layer_params(l, big, conv_dw_w, conv_dw_b, conv_ln_g, conv_ln_b, lru_conv_w, lru_conv_b, lru_wa,
                          lru_ba, lru_wx, lru_bx, lru_lambda, ln1_g, ln1_b, ffn_dw_w, ffn_dw_b, ln2_g, ln2_b)

        zc, zl, kt_all, vt_all, qtb, kb, vtb, kmean = _inproj_prompt(xp, p["w_in"], kt_all, vt_all, l, nbatch, seq)
        oab, cst, lst, hst = _mixer_prompt(zc, zl, p, nbatch, seq)
        madd = _select_prompt(rel_bias, _block_diag_means(kmean, nbatch, nblk, nbp), qtb, nbp)
        oc = _attn_prompt(qtb, kb, vtb, madd, t0, t1)
        x1 = _outproj(oab, oc, xp, p["w_out"], l, p["ln1_g"], p["ln1_b"])
        xp, fst = _ffn_prompt(x1, p, nbatch, seq)
        outs["conv_p"].append(cst)
        outs["lruc_p"].append(lst)
        outs["lruh_p"].append(hst.reshape(nbatch, LRU_CH))
        outs["ffn_p"].append(fst)

        zc, zl, q, k, v = _inproj_sample(xs, p["w_in"], l)
        oab, cst, lst, hst = _mixer_sample(zc, zl, state_conv[l].transpose(1, 0, 2),
                                           state_lru_conv[l].transpose(1, 0, 2), state_lru_h[l], p)
        q3, k3, v3 = (a.reshape(ns, 1, ATTN_WIDTH) for a in (q, k, v))
        sel = _kv_select(page_table, cache_kt, q3, l)
        sel = sel[:, :, :, 0].transpose(0, 2, 1).reshape(ns, ATTN_HEADS * MOBA_TOPK)
        oc = _attn_sample(page_table, sel, rel_bias, cache_kt, cache_vt, q3, k3, v3, bs, l)
        x1 = _outproj(oab, oc.reshape(ns, ATTN_WIDTH), xs, p["w_out"], l, p["ln1_g"], p["ln1_b"])
        xs, fst = _ffn_sample(x1, state_ffn_conv[l].transpose(1, 0, 2), p)
        outs["k_s"].append(k.reshape(ns, 1, ATTN_HEADS, HEAD_DIM))
        outs["v_s"].append(v.reshape(ns, 1, ATTN_HEADS, HEAD_DIM))
        outs["conv_s"].append(cst.transpose(1, 0, 2))
        outs["lruc_s"].append(lst.transpose(1, 0, 2))
        outs["lruh_s"].append(hst)
        outs["ffn_s"].append(fst.transpose(1, 0, 2))

    st = {k: jnp.stack(v) for k, v in outs.items()}
    to_heads = lambda a: a.reshape(depth, nbatch, ATTN_HEADS, HEAD_DIM, seq).transpose(0, 1, 4, 2, 3)
    return (xp.reshape(nbatch, seq, D_MODEL), xs.reshape(ns, 1, D_MODEL),
            to_heads(kt_all), to_heads(vt_all), st["k_s"], st["v_s"], st["conv_p"], st["conv_s"],
            st["lruc_p"], st["lruc_s"], st["lruh_p"], st["lruh_s"], st["ffn_p"], st["ffn_s"])
```
